```python
import math
import jax
import jax.numpy as jnp
from jax import lax

D_MODEL = 1024
BATCH = 8
SEQ = 4096
DEPTH = 2

CTX_LEN = 256
GRID_W = 64
D_SSD = 2 * D_MODEL
SSD_HEAD_DIM = 64
SSD_HEADS = D_SSD // SSD_HEAD_DIM
SSD_GROUPS = 4
SSD_STATE = 128
SSD_CHUNK = 128
CONV_WIDTH = 5
D_XBC = D_SSD + 2 * SSD_GROUPS * SSD_STATE
D_GM = D_MODEL
GM_CHUNK = 128
GM_GROUPS = 8
GM_GROUP_CH = D_GM // GM_GROUPS
N_BRANCH = 2
OFF_XBC = D_SSD
OFF_DT = OFF_XBC + D_XBC
OFF_U = OFF_DT + 2 * SSD_HEADS
OFF_V = OFF_U + D_GM
OFF_GATE = OFF_V + D_GM
D_IN = OFF_GATE + N_BRANCH * D_MODEL
D_FF = 2816
N_EXPERTS = 8
TOP_K = 2
D_FF_EXPERT = 3584
N_DENSE = (DEPTH + 1) // 2
N_MOE = DEPTH // 2
N_MOD = 6
EPS = 1e-6

kernel_name = 'hybrid_ssd_gmlp_moe_dit_block'


def rmsnorm(x, g):
    xf = x.astype(jnp.float32)
    y = xf * lax.rsqrt(jnp.mean(xf * xf, axis=-1, keepdims=True) + EPS)
    return (y * g.astype(jnp.float32)).astype(x.dtype)


def modulate(h, shift, scale):
    return h * (1 + scale) + shift


def dwconv_centred(u, w, b):
    k_w = w.shape[0]
    n = u.shape[-2]
    half = k_w // 2
    up = jnp.pad(u, [(0, 0)] * (u.ndim - 2) + [(half, half), (0, 0)])
    out = b
    for k in range(k_w):
        out = out + w[k] * up[..., k:k + n, :]
    return out


def ssd_chunked(xdt, a, b_in, c_in, h0, with_output):
    bsz, L, H, P = xdt.shape
    G, N = b_in.shape[-2:]
    nc = L // SSD_CHUNK
    Q = SSD_CHUNK
    hg = H // G
    xc = xdt.reshape(bsz, nc, Q, G, hg, P)
    bc = b_in.reshape(bsz, nc, Q, G, N)
    cc = c_in.reshape(bsz, nc, Q, G, N)
    acs = jnp.cumsum(a.astype(jnp.float32).reshape(bsz, nc, Q, G, hg), axis=2)
    a_tot = acs[:, :, -1]
    decay_to_end = jnp.exp(a_tot[:, :, None] - acs)
    chunk_states = jnp.einsum('bclgn,bclgh,bclghp->bcghpn', bc, decay_to_end, xc).astype(jnp.float32)

    def step(h, inp):
        s_c, at = inp
        return jnp.exp(at)[..., None, None] * h + s_c, h

    h_final, h_in = lax.scan(step, h0, (jnp.moveaxis(chunk_states, 1, 0), jnp.moveaxis(a_tot, 1, 0)))
    if not with_output:
        return None, h_final
    h_in = jnp.moveaxis(h_in, 0, 1)
    seg = acs[:, :, :, None] - acs[:, :, None, :]
    lower_tri = jnp.tril(jnp.ones((Q, Q), dtype=bool))[:, :, None, None]
    decay = jnp.exp(jnp.where(lower_tri, seg, -jnp.inf))
    cb = jnp.einsum('bclgn,bcsgn->bclsg', cc, bc)
    y_diag = jnp.einsum('bclsg,bclsgh,bcsghp->bclghp', cb, decay, xc)
    y_off = jnp.einsum('bclgn,bcghpn,bclgh->bclghp', cc, h_in, jnp.exp(acs))
    return (y_diag + y_off).reshape(bsz, L, H, P), h_final


def hybrid_mixer(h, lp, h0_f, h0_b, on_grid, with_output):
    bsz, L, _ = h.shape
    if with_output:
        z, xbc, dt, u, v, gl = jnp.split(h @ lp['w_in'], [OFF_XBC, OFF_DT, OFF_U, OFF_V, OFF_GATE], axis=-1)
    else:
        xbc, dt = jnp.split(h @ lp['w_in'][:, OFF_XBC:OFF_U], [D_XBC], axis=-1)
    if on_grid:
        rows = L // GRID_W
        xbc = dwconv_centred(xbc.reshape(bsz, rows, GRID_W, D_XBC), lp['conv_w'], lp['conv_b']).reshape(bsz, L, D_XBC)
    else:
        xbc = dwconv_centred(xbc, lp['conv_w'], lp['conv_b'])
    xbc = jax.nn.silu(xbc)
    xs, bm, cm = jnp.split(xbc, [D_SSD, D_SSD + SSD_GROUPS * SSD_STATE], axis=-1)
    xs = xs.reshape(bsz, L, SSD_HEADS, SSD_HEAD_DIM)
    bm = bm.reshape(bsz, L, SSD_GROUPS, SSD_STATE)
    cm = cm.reshape(bsz, L, SSD_GROUPS, SSD_STATE)
    dt = jax.nn.softplus(dt.astype(jnp.float32).reshape(bsz, L, 2, SSD_HEADS) + lp['dt_bias'].astype(jnp.float32))
    a_neg = -jnp.exp(lp['a_log'].astype(jnp.float32))
    flip = lambda t: jnp.flip(t, axis=1)
    y_f, hf_f = ssd_chunked(xs * dt[:, :, 0, :, None], dt[:, :, 0] * a_neg[0], bm, cm, h0_f, with_output)
    y_b, hf_b = ssd_chunked(flip(xs * dt[:, :, 1, :, None]), flip(dt[:, :, 1] * a_neg[1]), flip(bm), flip(cm), h0_b, with_output)
    if not with_output:
        return None, hf_f, hf_b
    y = (y_f + flip(y_b) + xs * lp['d_skip'][:, None]).reshape(bsz, L, D_SSD).astype(h.dtype)
    y_ssd = rmsnorm(y * jax.nn.silu(z), lp['ssd_norm_g'])
    u = jax.nn.gelu(u)
    v = rmsnorm(jax.nn.gelu(v), lp['gm_norm_g'])
    v = v.reshape(bsz, L // GM_CHUNK, GM_CHUNK, GM_GROUPS, GM_GROUP_CH)
    s = jnp.einsum('gts,bnsgc->bntgc', lp['w_spatial'], v) + lp['b_spatial'].T[None, None, :, :, None]
    y_gm = u * s.reshape(bsz, L, D_GM)
    gates = jax.nn.sigmoid(gl.astype(jnp.float32)).astype(h.dtype).reshape(bsz, L, N_BRANCH, D_MODEL)
    merged = gates[:, :, 0] * (y_ssd @ lp['w_ssd_br']) + gates[:, :, 1] * (y_gm @ lp['w_gm_br'])
    return merged @ lp['w_out'], hf_f, hf_b


def swiglu(h, w_gate, w_up, w_down):
    return (jax.nn.silu(h @ w_gate) * (h @ w_up)) @ w_down


def moe_swiglu(h, w_router, b_router, w_gate, w_up, w_down):
    shp = h.shape
    t = h.reshape(-1, shp[-1])
    logits = (t @ w_router).astype(jnp.float32) + b_router.astype(jnp.float32)
    top_val, top_idx = lax.top_k(logits, TOP_K)
    top_p = jax.nn.softmax(top_val, axis=-1)
    combine = jnp.einsum('tk,tke->te', top_p, jax.nn.one_hot(top_idx, N_EXPERTS, dtype=jnp.float32)).astype(h.dtype)
    out = jnp.zeros_like(t)
    for e in range(N_EXPERTS):
        out = out + combine[:, e:e + 1] * swiglu(t, w_gate[e], w_up[e], w_down[e])
    return out.reshape(shp)


def channel_mixer(h, i, ffn_w_gate, ffn_w_up, ffn_w_down, moe_router_w, moe_router_b, moe_w_gate, moe_w_up, moe_w_down):
    j = i // 2
    if i % 2 == 0:
        return swiglu(h, ffn_w_gate[j], ffn_w_up[j], ffn_w_down[j])
    return moe_swiglu(h, moe_router_w[j], moe_router_b[j], moe_w_gate[j], moe_w_up[j], moe_w_down[j])


def setup_inputs(seed: int = 0) -> dict:
    key = jax.random.key(seed)
    ks = jax.random.split(key, 32)
    f32 = jnp.float32

    def nrm(k, shape, fan_in):
        return jax.random.normal(k, shape, f32) * (fan_in ** -0.5)

    def gain(k, shape):
        return 1.0 + 0.02 * jax.random.normal(k, shape, f32)

    dt0 = jnp.exp(jax.random.uniform(ks[9], (DEPTH, 2, SSD_HEADS), f32, math.log(1e-3), math.log(1e-1)))
    return {
        'x': jax.random.normal(ks[0], (BATCH, SEQ, D_MODEL), f32),
        'c': jax.random.normal(ks[1], (BATCH, D_MODEL), f32),
        'ctx': jax.random.normal(ks[2], (BATCH, CTX_LEN, D_MODEL), f32),
        'c_ctx': jax.random.normal(ks[3], (D_MODEL,), f32),
        'w_ada': nrm(ks[4], (DEPTH, D_MODEL, N_MOD * D_MODEL), D_MODEL),
        'b_ada': 0.02 * jax.random.normal(ks[5], (DEPTH, N_MOD * D_MODEL), f32),
        'norm1_g': gain(ks[6], (DEPTH, D_MODEL)),
        'w_in': nrm(ks[7], (DEPTH, D_MODEL, D_IN), D_MODEL),
        'conv_w': nrm(ks[8], (DEPTH, CONV_WIDTH, D_XBC), CONV_WIDTH),
        'conv_b': 0.02 * jax.random.normal(ks[10], (DEPTH, D_XBC), f32),
        'dt_bias': dt0 + jnp.log(-jnp.expm1(-dt0)),
        'a_log': jnp.log(jax.random.uniform(ks[11], (DEPTH, 2, SSD_HEADS), f32, 1.0, 16.0)),
        'd_skip': gain(ks[12], (DEPTH, SSD_HEADS)),
        'ssd_norm_g': gain(ks[13], (DEPTH, D_SSD)),
        'w_ssd_br': nrm(ks[14], (DEPTH, D_SSD, D_MODEL), D_SSD),
        'gm_norm_g': gain(ks[15], (DEPTH, D_GM)),
        'w_spatial': nrm(ks[16], (DEPTH, GM_GROUPS, GM_CHUNK, GM_CHUNK), GM_CHUNK),
        'b_spatial': gain(ks[17], (DEPTH, GM_GROUPS, GM_CHUNK)),
        'w_gm_br': nrm(ks[18], (DEPTH, D_GM, D_MODEL), D_GM),
        'w_out': nrm(ks[19], (DEPTH, D_MODEL, D_MODEL), D_MODEL),
        'norm2_g': gain(ks[20], (DEPTH, D_MODEL)),
        'ffn_w_gate': nrm(ks[21], (N_DENSE, D_MODEL, D_FF), D_MODEL),
        'ffn_w_up': nrm(ks[22], (N_DENSE, D_MODEL, D_FF), D_MODEL),
        'ffn_w_down': nrm(ks[23], (N_DENSE, D_FF, D_MODEL), D_FF),
        'moe_router_w': nrm(ks[24], (N_MOE, D_MODEL, N_EXPERTS), D_MODEL),
        'moe_router_b': 0.01 * jax.random.normal(ks[25], (N_MOE, N_EXPERTS), f32),
        'moe_w_gate': nrm(ks[26], (N_MOE, N_EXPERTS, D_MODEL, D_FF_EXPERT), D_MODEL),
        'moe_w_up': nrm(ks[27], (N_MOE, N_EXPERTS, D_MODEL, D_FF_EXPERT), D_MODEL),
        'moe_w_down': nrm(ks[28], (N_MOE, N_EXPERTS, D_FF_EXPERT, D_MODEL), D_FF_EXPERT),
        'final_norm_g': gain(ks[29], (D_MODEL,)),
    }


def reference(x, c, ctx, c_ctx, w_ada, b_ada, norm1_g, w_in, conv_w, conv_b, dt_bias, a_log,
              d_skip, ssd_norm_g, w_ssd_br, gm_norm_g, w_spatial, b_spatial, w_gm_br, w_out,
              norm2_g, ffn_w_gate, ffn_w_up, ffn_w_down, moe_router_w, moe_router_b,
              moe_w_gate, moe_w_up, moe_w_down, final_norm_g):
    bsz = x.shape[0]
    h_zero = jnp.zeros((bsz, SSD_GROUPS, SSD_HEADS // SSD_GROUPS, SSD_HEAD_DIM, SSD_STATE), jnp.float32)
    for i in range(DEPTH):
        last = i == DEPTH - 1
        lp = dict(w_in=w_in[i], conv_w=conv_w[i], conv_b=conv_b[i], dt_bias=dt_bias[i], a_log=a_log[i],
                  d_skip=d_skip[i], ssd_norm_g=ssd_norm_g[i], w_ssd_br=w_ssd_br[i], gm_norm_g=gm_norm_g[i],
                  w_spatial=w_spatial[i], b_spatial=b_spatial[i], w_gm_br=w_gm_br[i], w_out=w_out[i])
        mod_x = (jax.nn.silu(c) @ w_ada[i] + b_ada[i]).reshape(bsz, N_MOD, 1, D_MODEL)
        mod_c = (jax.nn.silu(c_ctx) @ w_ada[i] + b_ada[i]).reshape(N_MOD, 1, D_MODEL)
        hc = modulate(rmsnorm(ctx, norm1_g[i]), mod_c[0], mod_c[1])
        ctx_mix, st_f, st_b = hybrid_mixer(hc, lp, h_zero, h_zero, False, not last)
        hx = modulate(rmsnorm(x, norm1_g[i]), mod_x[:, 0], mod_x[:, 1])
        x_mix, _, _ = hybrid_mixer(hx, lp, st_f, st_b, True, True)
        x = x + mod_x[:, 2] * x_mix
        hx = modulate(rmsnorm(x, norm2_g[i]), mod_x[:, 3], mod_x[:, 4])
        x = x + mod_x[:, 5] * channel_mixer(hx, i, ffn_w_gate, ffn_w_up, ffn_w_down, moe_router_w,
                                            moe_router_b, moe_w_gate, moe_w_up, moe_w_down)
        if not last:
            ctx = ctx + mod_c[2] * ctx_mix
            hc = modulate(rmsnorm(ctx, norm2_g[i]), mod_c[3], mod_c[4])
            ctx = ctx + mod_c[5] * channel_mixer(hc, i, ffn_w_gate, ffn_w_up, ffn_w_down, moe_router_w,
                                                moe_router_b, moe_w_gate, moe_w_up, moe_w_down)
    return rmsnorm(x, final_norm_g)
```

```python
import functools

import jax
import jax.numpy as jnp
from jax import lax
from jax.experimental import pallas as pl
from jax.experimental.pallas import tpu as pltpu

F32 = jnp.float32
BF16 = jnp.bfloat16
EPS = 1e-6

SSD_HEAD_DIM = 64
SSD_HEADS = 32
SSD_GROUPS = 4
SSD_STATE = 128
HEADS_PER_GROUP = SSD_HEADS // SSD_GROUPS
GROUP_CH = HEADS_PER_GROUP * SSD_HEAD_DIM
CONV_WIDTH = 5
GRID_W = 64
GM_CHUNK = 128
GM_GROUPS = 8
N_EXPERTS = 8
TOP_K = 2
N_MOD = 6

LANES = 128
SSD_Q = 128
NEG_BIG = -1e30

VMEM_LIMIT = 56 * 1024 * 1024


def _cparams(sem, vmem=VMEM_LIMIT):
    return pltpu.CompilerParams(dimension_semantics=sem, vmem_limit_bytes=vmem)


def _resident(shape):
    nd = len(shape)
    return pl.BlockSpec(shape, lambda *_: (0,) * nd, pipeline_mode=pl.Buffered(1))


def _silu(v):
    return v * jax.nn.sigmoid(v)


def _gelu_tanh(v):
    return 0.5 * v * (1.0 + jnp.tanh(0.7978845608028654 * (v + 0.044715 * v * v * v)))


def _softplus(v):
    return jnp.maximum(v, 0.0) + jnp.log1p(jnp.exp(-jnp.abs(v)))


def _rms_scale(v):
    return lax.rsqrt(jnp.mean(v * v, axis=-1, keepdims=True) + EPS)


def _dot(a, b):
    return jnp.dot(a, b, preferred_element_type=F32)


def _mod_kernel(cc_ref, w_ref, b_ref, o_ref):
    s = _silu(cc_ref[...])
    o_ref[0] = _dot(s.astype(BF16), w_ref[0].astype(BF16)) + b_ref[0]


def _mod_call(cc, w_ada, b_ada):
    depth, d, n = w_ada.shape
    tn = 1536
    rows = cc.shape[0]
    return pl.pallas_call(
        _mod_kernel,
        grid=(depth, n // tn),
        in_specs=[pl.BlockSpec((rows, d), lambda i, j: (0, 0)),
                  pl.BlockSpec((1, d, tn), lambda i, j: (i, 0, j)),
                  pl.BlockSpec((1, 1, tn), lambda i, j: (i, 0, j))],
        out_specs=pl.BlockSpec((1, rows, tn), lambda i, j: (i, 0, j)),
        out_shape=jax.ShapeDtypeStruct((depth, rows, n), F32),
        compiler_params=_cparams(("arbitrary", "arbitrary")),
        name="mod",
    )(cc, w_ada, b_ada.reshape(depth, 1, n))


IN_CHUNK = 512


def _in_proj_kernel(x_ref, sh_ref, sc_ref, g_ref, w_ref, cw_ref, cb_ref, dtb_ref, gmg_ref,
                    *out_refs, seg, ssd_only, d_ssd, d_xbc, d_gm, d_model):
    x = x_ref[0]
    tm = x.shape[0]
    h = x * _rms_scale(x) * g_ref[...]
    h = h * (1.0 + sc_ref[0]) + sh_ref[0]
    hb = h.astype(BF16)

    if ssd_only:
        xbc_ref, dt_ref = out_refs
        off_xbc = 0
    else:
        z_ref, xbc_ref, dt_ref, u_ref, v_ref, gt_ref = out_refs
        off_xbc = d_ssd
        for c in range(0, d_ssd, IN_CHUNK):
            acc = _dot(hb, w_ref[:, c:c + IN_CHUNK])
            z_ref[0, :, c:c + IN_CHUNK] = _silu(acc).astype(BF16)

    rpos = lax.broadcasted_iota(jnp.int32, (tm, 1), 0) & (seg - 1)
    for c in range(0, d_xbc, IN_CHUNK):
        cs = slice(c, c + IN_CHUNK)
        acc = _dot(hb, w_ref[:, off_xbc + c:off_xbc + c + IN_CHUNK])
        out = cb_ref[:, cs] + cw_ref[CONV_WIDTH // 2:CONV_WIDTH // 2 + 1, cs] * acc
        for k in range(CONV_WIDTH):
            d = k - CONV_WIDTH // 2
            if d == 0:
                continue
            shifted = pltpu.roll(acc, (-d) % tm, axis=0)
            valid = (rpos + d >= 0) & (rpos + d < seg)
            out = out + cw_ref[k:k + 1, cs] * jnp.where(valid, shifted, 0.0)
        xbc_ref[0, :, cs] = _silu(out).astype(BF16)

    off = off_xbc + d_xbc
    if not ssd_only:
        for c in range(0, d_gm, IN_CHUNK):
            acc = _dot(hb, w_ref[:, off + c:off + c + IN_CHUNK])
            u_ref[0, :, c:c + IN_CHUNK] = _gelu_tanh(acc).astype(BF16)
        off += d_gm
        gv = _gelu_tanh(_dot(hb, w_ref[:, off:off + d_gm]))
        v_ref[0] = (gv * _rms_scale(gv) * gmg_ref[...]).astype(BF16)
        off += d_gm
        for c in range(0, 2 * d_model, IN_CHUNK):
            acc = _dot(hb, w_ref[:, off + c:off + c + IN_CHUNK])
            gt_ref[0, :, c:c + IN_CHUNK] = jax.nn.sigmoid(acc).astype(BF16)
        off += 2 * d_model
    dt_ref[0] = _softplus(_dot(hb, w_ref[:, off:off + 2 * LANES]) + dtb_ref[...])


def _in_proj_call(x, shift, scale, n1g, w_perm, conv_w, conv_b, dtb_pad, gm_g, *, seg, tm, ssd_only,
                  d_ssd, d_xbc, d_gm):
    bsz, length, d_model = x.shape
    row = lambda width: pl.BlockSpec((1, tm, width), lambda b, i: (b, i, 0))
    vec = pl.BlockSpec((1, 1, d_model), lambda b, i: (b, 0, 0))
    shp = lambda width, dt: jax.ShapeDtypeStruct((bsz, length, width), dt)
    if ssd_only:
        out_specs = [row(d_xbc), row(2 * LANES)]
        out_shape = [shp(d_xbc, BF16), shp(2 * LANES, F32)]
    else:
        out_specs = [row(d_ssd), row(d_xbc), row(2 * LANES), row(d_gm), row(d_gm), row(2 * d_model)]
        out_shape = [shp(d_ssd, BF16), shp(d_xbc, BF16), shp(2 * LANES, F32), shp(d_gm, BF16),
                     shp(d_gm, BF16), shp(2 * d_model, BF16)]
    kern = functools.partial(_in_proj_kernel, seg=seg, ssd_only=ssd_only, d_ssd=d_ssd, d_xbc=d_xbc,
                             d_gm=d_gm, d_model=d_model)
    return pl.pallas_call(
        kern,
        grid=(bsz, length // tm),
        in_specs=[row(d_model), vec, vec, _resident(n1g.shape), _resident(w_perm.shape),
                  _resident(conv_w.shape), _resident(conv_b.shape), _resident(dtb_pad.shape),
                  _resident(gm_g.shape)],
        out_specs=out_specs,
        out_shape=out_shape,
        compiler_params=_cparams(("arbitrary", "arbitrary")),
        name="in_proj_ssd_only" if ssd_only else "in_proj",
    )(x, shift, scale, n1g, w_perm, conv_w, conv_b, dtb_pad, gm_g)


def _split3(v):
    v1 = v.astype(BF16)
    r1 = v - v1.astype(F32)
    v2 = r1.astype(BF16)
    v3 = (r1 - v2.astype(F32)).astype(BF16)
    return v1, v2, v3


def _ssd_direction(x_ref, b_ref, c_ref, dt_ref, aneg, e_ref, dskip_ref, s_ref, y_ref, *, reverse,
                   with_output):
    q = SSD_Q
    x = x_ref[0]
    dt = dt_ref[0]
    a = dt * aneg
    row = lax.broadcasted_iota(jnp.int32, (q, q), 0)
    col = lax.broadcasted_iota(jnp.int32, (q, q), 1)
    tri = (row <= col) if reverse else (row >= col)
    tmat = tri.astype(F32).astype(BF16)
    a1, a2, a3 = _split3(a)
    acs = _dot(tmat, a1) + _dot(tmat, a2) + _dot(tmat, a3)
    a_tot = acs[0:1] if reverse else acs[q - 1:q]
    eacs = jnp.exp(acs)
    w = jnp.exp(a_tot - acs) * dt
    eat = jnp.broadcast_to(jnp.exp(a_tot), (8, LANES))
    stack = jnp.concatenate([w, eacs, eat], axis=0)
    hi = stack.astype(BF16)
    lo = (stack - hi.astype(F32)).astype(BF16)
    expd = _dot(hi, e_ref[...]) + _dot(lo, e_ref[...])
    w_exp = expd[0:q]
    eacs_exp = expd[q:2 * q]
    eat_exp = expd[2 * q:2 * q + 1]

    xw = (x.astype(F32) * w_exp).astype(BF16)

    if with_output:
        acs_t = acs.T
        ldt_t = jnp.log(dt).T
        lane = lax.broadcasted_iota(jnp.int32, (q, LANES), 1)
        lo_half = lane < SSD_HEAD_DIM
        for g in range(SSD_GROUPS):
            cg = c_ref[0, :, g * SSD_STATE:(g + 1) * SSD_STATE]
            bg = b_ref[0, :, g * SSD_STATE:(g + 1) * SSD_STATE]
            cb = lax.dot_general(cg, bg, (((1,), (1,)), ((), ())), preferred_element_type=F32)
            sg = s_ref[g].astype(BF16)
            y_off = _dot(cg, sg) * eacs_exp[:, g * GROUP_CH:(g + 1) * GROUP_CH]
            for j in range(HEADS_PER_GROUP // 2):
                h0 = g * HEADS_PER_GROUP + 2 * j
                ms = []
                for hh in (h0, h0 + 1):
                    expo = acs[:, hh:hh + 1] + (ldt_t[hh:hh + 1, :] - acs_t[hh:hh + 1, :])
                    ms.append((cb * jnp.exp(jnp.where(tri, expo, NEG_BIG))).astype(BF16))
                lhs = jnp.concatenate(ms, axis=1)
                cs = slice(h0 * SSD_HEAD_DIM, (h0 + 2) * SSD_HEAD_DIM)
                xp = x[:, cs]
                zero = jnp.zeros_like(xp)
                rhs = jnp.concatenate([jnp.where(lo_half, xp, zero), jnp.where(lo_half, zero, xp)],
                                      axis=0)
                y = _dot(lhs, rhs) + y_off[:, 2 * j * SSD_HEAD_DIM:(2 * j + 2) * SSD_HEAD_DIM]
                if dskip_ref is not None:
                    y = y + xp.astype(F32) * dskip_ref[:, cs]
                y_ref[0, :, cs] = y.astype(y_ref.dtype)

    for g in range(SSD_GROUPS):
        bg = b_ref[0, :, g * SSD_STATE:(g + 1) * SSD_STATE]
        gs = slice(g * GROUP_CH, (g + 1) * GROUP_CH)
        new = lax.dot_general(bg, xw[:, gs], (((0,), (0,)), ((), ())), preferred_element_type=F32)
        s_ref[g] = s_ref[g] * eat_exp[:, gs] + new


def _ssd_kernel(xf_ref, bf_ref, cf_ref, dtf_ref, xb_ref, bb_ref, cb_ref, dtb_ref, alog_ref, e_ref,
                dskip_ref, h0f_ref, h0b_ref, *rest, with_output):
    if with_output:
        yf_ref, yb_ref, hff_ref, hfb_ref, sf_ref, sb_ref = rest
    else:
        hff_ref, hfb_ref, sf_ref, sb_ref = rest
        yf_ref = yb_ref = None
    c = pl.program_id(1)

    @pl.when(c == 0)
    def _():
        sf_ref[...] = h0f_ref[0]
        sb_ref[...] = h0b_ref[0]

    lane = lax.broadcasted_iota(jnp.int32, (1, LANES), 1)
    aneg = jnp.where(lane < SSD_HEADS, -jnp.exp(alog_ref[...]), 0.0)
    _ssd_direction(xf_ref, bf_ref, cf_ref, dtf_ref, aneg[0:1], e_ref, dskip_ref, sf_ref, yf_ref,
                   reverse=False, with_output=with_output)
    _ssd_direction(xb_ref, bb_ref, cb_ref, dtb_ref, aneg[1:2], e_ref, None, sb_ref, yb_ref,
                   reverse=True, with_output=with_output)

    @pl.when(c == pl.num_programs(1) - 1)
    def _():
        hff_ref[0] = sf_ref[...]
        hfb_ref[0] = sb_ref[...]


def _ssd_call(xbc, dt, alog_pad, e_mat, dskip_exp, h0f, h0b, *, with_output, d_ssd):
    bsz, length, d_xbc = xbc.shape
    q = SSD_Q
    nc = length // q
    nb = d_ssd // (SSD_GROUPS * SSD_STATE)
    fwd = lambda b, c: c
    bwd = lambda b, c: nc - 1 - c

    def chunk_specs(pos, dt_block):
        return [pl.BlockSpec((1, q, d_ssd), lambda b, c: (b, pos(b, c), 0)),
                pl.BlockSpec((1, q, SSD_GROUPS * SSD_STATE), lambda b, c: (b, pos(b, c), nb)),
                pl.BlockSpec((1, q, SSD_GROUPS * SSD_STATE), lambda b, c: (b, pos(b, c), nb + 1)),
                pl.BlockSpec((1, q, LANES), lambda b, c: (b, pos(b, c), dt_block))]

    st_shape = (SSD_GROUPS, SSD_STATE, GROUP_CH)
    st_spec = pl.BlockSpec((1,) + st_shape, lambda b, c: (b, 0, 0, 0))
    in_specs = (chunk_specs(fwd, 0) + chunk_specs(bwd, 1)
                + [_resident(alog_pad.shape), _resident(e_mat.shape), _resident(dskip_exp.shape),
                   st_spec, st_spec])
    st_out = jax.ShapeDtypeStruct((bsz,) + st_shape, F32)
    out_specs = [st_spec, st_spec]
    out_shape = [st_out, st_out]
    if with_output:
        y_out = jax.ShapeDtypeStruct((bsz, length, d_ssd), BF16)
        out_specs = [pl.BlockSpec((1, q, d_ssd), lambda b, c: (b, c, 0)),
                     pl.BlockSpec((1, q, d_ssd), lambda b, c: (b, nc - 1 - c, 0))] + out_specs
        out_shape = [y_out, y_out] + out_shape
    return pl.pallas_call(
        functools.partial(_ssd_kernel, with_output=with_output),
        grid=(bsz, nc),
        in_specs=in_specs,
        out_specs=out_specs,
        out_shape=out_shape,
        scratch_shapes=[pltpu.VMEM(st_shape, F32), pltpu.VMEM(st_shape, F32)],
        compiler_params=_cparams(("arbitrary", "arbitrary")),
        name="ssd" if with_output else "ssd_states",
    )(xbc, xbc, xbc, dt, xbc, xbc, xbc, dt, alog_pad, e_mat, dskip_exp, h0f, h0b)


def _merge_kernel(yf_ref, yb_ref, z_ref, u_ref, v_ref, gt_ref, x_ref, gate_ref, sg_ref, wssd_ref,
                  wgm_ref, wout_ref, wsp_ref, bsp_ref, o_ref, ygm_ref):
    tm = x_ref.shape[1]
    d_model = x_ref.shape[2]
    yz = (yf_ref[0].astype(F32) + yb_ref[0].astype(F32)) * z_ref[0].astype(F32)
    y_ssd = (yz * _rms_scale(yz) * sg_ref[...]).astype(BF16)
    br_ssd = _dot(y_ssd, wssd_ref[...])
    gch = d_model // GM_GROUPS
    for n in range(tm // GM_CHUNK):
        rs = slice(n * GM_CHUNK, (n + 1) * GM_CHUNK)
        for g in range(GM_GROUPS):
            cs = slice(g * gch, (g + 1) * gch)
            s = _dot(wsp_ref[g], v_ref[0, rs, cs]) + bsp_ref[:, cs]
            ygm_ref[rs, cs] = (u_ref[0, rs, cs].astype(F32) * s).astype(BF16)
    br_gm = _dot(ygm_ref[...], wgm_ref[...])
    merged = (gt_ref[0, :, :d_model].astype(F32) * br_ssd
              + gt_ref[0, :, d_model:].astype(F32) * br_gm).astype(BF16)
    o_ref[0] = x_ref[0] + gate_ref[0] * _dot(merged, wout_ref[...])


def _merge_call(yf, yb, z, u, v, gt, x, gate, ssd_g, w_ssd, w_gm, w_out, w_sp, bsp_exp, *, tm):
    bsz, length, d_model = x.shape
    row = lambda arr: pl.BlockSpec((1, tm, arr.shape[2]), lambda b, i: (b, i, 0))
    vec = pl.BlockSpec((1, 1, d_model), lambda b, i: (b, 0, 0))
    return pl.pallas_call(
        _merge_kernel,
        grid=(bsz, length // tm),
        in_specs=[row(yf), row(yb), row(z), row(u), row(v), row(gt), row(x), vec,
                  _resident(ssd_g.shape), _resident(w_ssd.shape), _resident(w_gm.shape),
                  _resident(w_out.shape), _resident(w_sp.shape), _resident(bsp_exp.shape)],
        out_specs=row(x),
        out_shape=jax.ShapeDtypeStruct(x.shape, F32),
        scratch_shapes=[pltpu.VMEM((tm, d_model), BF16)],
        compiler_params=_cparams(("arbitrary", "arbitrary")),
        name="merge",
    )(yf, yb, z, u, v, gt, x, gate, ssd_g, w_ssd, w_gm, w_out, w_sp, bsp_exp)


FFN_CHUNK = 256


def _ffn_kernel(x_ref, sh_ref, sc_ref, gate_ref, g_ref, wg_ref, wu_ref, wd_ref, o_ref):
    x = x_ref[0]
    h = x * _rms_scale(x) * g_ref[...]
    hb = (h * (1.0 + sc_ref[0]) + sh_ref[0]).astype(BF16)
    d_ff = wg_ref.shape[1]
    acc = jnp.zeros(x.shape, F32)
    step = 4 * FFN_CHUNK
    for c in range(0, d_ff, step):
        wdt = min(step, d_ff - c)
        act = (_silu(_dot(hb, wg_ref[:, c:c + wdt])) * _dot(hb, wu_ref[:, c:c + wdt])).astype(BF16)
        acc = acc + _dot(act, wd_ref[c:c + wdt, :])
    o_ref[0] = x + gate_ref[0] * acc


def _ffn_call(x, shift, scale, gate, n2g, w_gate, w_up, w_down, *, tm):
    bsz, length, d_model = x.shape
    row = pl.BlockSpec((1, tm, d_model), lambda b, i: (b, i, 0))
    vec = pl.BlockSpec((1, 1, d_model), lambda b, i: (b, 0, 0))
    return pl.pallas_call(
        _ffn_kernel,
        grid=(bsz, length // tm),
        in_specs=[row, vec, vec, vec, _resident(n2g.shape), _resident(w_gate.shape),
                  _resident(w_up.shape), _resident(w_down.shape)],
        out_specs=row,
        out_shape=jax.ShapeDtypeStruct(x.shape, F32),
        compiler_params=_cparams(("arbitrary", "arbitrary")),
        name="ffn",
    )(x, shift, scale, gate, n2g, w_gate, w_up, w_down)


def _router_kernel(x_ref, sh_ref, sc_ref, g_ref, wr_ref, br_ref, h_ref, idx_ref, p_ref):
    x = x_ref[0]
    h = x * _rms_scale(x) * g_ref[...]
    h = h * (1.0 + sc_ref[0]) + sh_ref[0]
    h_ref[...] = h
    logits = _dot(h.astype(BF16), wr_ref[...]) + br_ref[...]
    eidx = lax.broadcasted_iota(jnp.int32, logits.shape, 1).astype(F32)
    no_expert = float(N_EXPERTS)
    m1 = jnp.max(logits, axis=-1, keepdims=True)
    i1 = jnp.min(jnp.where(logits == m1, eidx, no_expert), axis=-1, keepdims=True)
    rest = jnp.where(eidx == i1, -jnp.inf, logits)
    m2 = jnp.max(rest, axis=-1, keepdims=True)
    i2 = jnp.min(jnp.where(rest == m2, eidx, no_expert), axis=-1, keepdims=True)
    e2 = jnp.exp(m2 - m1)
    p1 = 1.0 / (1.0 + e2)
    first = lax.broadcasted_iota(jnp.int32, idx_ref.shape, 1) == 0
    idx_ref[...] = jnp.where(first, i1, i2).astype(jnp.int32)
    p_ref[...] = jnp.where(first, p1, e2 * p1)


def _router_call(x, shift, scale, n2g, w_router, b_router, *, tm):
    bsz, length, d_model = x.shape
    nt = length // tm
    tokens = bsz * length
    row = pl.BlockSpec((1, tm, d_model), lambda b, i: (b, i, 0))
    vec = pl.BlockSpec((1, 1, d_model), lambda b, i: (b, 0, 0))
    flat = lambda width: pl.BlockSpec((tm, width), lambda b, i: (b * nt + i, 0))
    return pl.pallas_call(
        _router_kernel,
        grid=(bsz, nt),
        in_specs=[row, vec, vec, _resident(n2g.shape), _resident(w_router.shape),
                  _resident(b_router.shape)],
        out_specs=[flat(d_model), flat(TOP_K), flat(TOP_K)],
        out_shape=[jax.ShapeDtypeStruct((tokens, d_model), F32),
                   jax.ShapeDtypeStruct((tokens, TOP_K), jnp.int32),
                   jax.ShapeDtypeStruct((tokens, TOP_K), F32)],
        compiler_params=_cparams(("arbitrary", "arbitrary")),
        name="router",
    )(x, shift, scale, n2g, w_router, b_router)


DISPATCH_CHUNK = 2048


def _dispatch_kernel(pos_ref, h_hbm, init_hbm, xs_hbm, sem, *, tokens):
    del init_hbm
    i = pl.program_id(0)
    base = (i * DISPATCH_CHUNK) % tokens

    def copy(jj):
        return pltpu.make_async_copy(h_hbm.at[pl.ds(base + jj, 1)], xs_hbm.at[pl.ds(pos_ref[jj], 1)],
                                     sem)

    def start(jj, carry):
        copy(jj).start()
        return carry

    def wait(jj, carry):
        copy(jj).wait()
        return carry

    lax.fori_loop(0, DISPATCH_CHUNK, start, 0)
    lax.fori_loop(0, DISPATCH_CHUNK, wait, 0)


def _dispatch_call(pos, h, rows):
    tokens, d_model = h.shape
    slots = pos.shape[0]
    init = jnp.zeros((rows, d_model), F32)
    return pl.pallas_call(
        functools.partial(_dispatch_kernel, tokens=tokens),
        grid=(slots // DISPATCH_CHUNK,),
        in_specs=[pl.BlockSpec((DISPATCH_CHUNK,), lambda i: (i,), memory_space=pltpu.SMEM),
                  pl.BlockSpec(memory_space=pl.ANY), pl.BlockSpec(memory_space=pl.ANY)],
        out_specs=pl.BlockSpec(memory_space=pl.ANY),
        out_shape=jax.ShapeDtypeStruct((rows, d_model), F32),
        scratch_shapes=[pltpu.SemaphoreType.DMA],
        input_output_aliases={2: 0},
        compiler_params=_cparams(("arbitrary",)),
        name="moe_dispatch",
    )(pos, h, init)


MOE_TM = 512
MOE_TF = 512


def _experts_kernel(gid_ref, nt_ref, xs_ref, wg_ref, wu_ref, wd_ref, ys_ref, xb_ref, acc_ref):
    m = pl.program_id(0)
    f = pl.program_id(1)

    @pl.when(m < nt_ref[0])
    def _():
        @pl.when(f == 0)
        def _():
            xb_ref[...] = xs_ref[...].astype(BF16)

        xb = xb_ref[...]
        act = (_silu(_dot(xb, wg_ref[0])) * _dot(xb, wu_ref[0])).astype(BF16)
        part = _dot(act, wd_ref[0])

        @pl.when(f == 0)
        def _():
            acc_ref[...] = part

        @pl.when(f > 0)
        def _():
            acc_ref[...] += part

        @pl.when(f == pl.num_programs(1) - 1)
        def _():
            ys_ref[...] = acc_ref[...]


def _experts_call(tile_gid, n_tiles, xs, w_gate, w_up, w_down):
    rows, d_model = xs.shape
    d_ff = w_gate.shape[2]
    nt_max = rows // MOE_TM
    nf = d_ff // MOE_TF

    def tile(m, nt):
        return jnp.minimum(m, nt[0] - 1)

    def fcol(m, f, nt):
        return jnp.where(m < nt[0], f, nf - 1)

    grid_spec = pltpu.PrefetchScalarGridSpec(
        num_scalar_prefetch=2,
        grid=(nt_max, nf),
        in_specs=[pl.BlockSpec((MOE_TM, d_model), lambda m, f, gid, nt: (tile(m, nt), 0)),
                  pl.BlockSpec((1, d_model, MOE_TF),
                               lambda m, f, gid, nt: (gid[tile(m, nt)], 0, fcol(m, f, nt))),
                  pl.BlockSpec((1, d_model, MOE_TF),
                               lambda m, f, gid, nt: (gid[tile(m, nt)], 0, fcol(m, f, nt))),
                  pl.BlockSpec((1, MOE_TF, d_model),
                               lambda m, f, gid, nt: (gid[tile(m, nt)], fcol(m, f, nt), 0))],
        out_specs=pl.BlockSpec((MOE_TM, d_model), lambda m, f, gid, nt: (tile(m, nt), 0)),
        scratch_shapes=[pltpu.VMEM((MOE_TM, d_model), BF16), pltpu.VMEM((MOE_TM, d_model), F32)],
    )
    return pl.pallas_call(
        _experts_kernel,
        grid_spec=grid_spec,
        out_shape=jax.ShapeDtypeStruct((rows, d_model), F32),
        compiler_params=_cparams(("arbitrary", "arbitrary")),
        name="moe_experts",
    )(tile_gid, n_tiles, xs, w_gate, w_up, w_down)


COMBINE_TM = 256


def _combine_kernel(pos1_ref, pos2_ref, ys_hbm, x_ref, p_ref, gate_ref, fg_ref, o_ref, buf1, buf2, sem):
    def copy(jj, pos_ref, buf):
        return pltpu.make_async_copy(ys_hbm.at[pl.ds(pos_ref[jj], 1)], buf.at[pl.ds(jj, 1)], sem)

    def start(jj, carry):
        copy(jj, pos1_ref, buf1).start()
        copy(jj, pos2_ref, buf2).start()
        return carry

    def wait(jj, carry):
        copy(jj, pos1_ref, buf1).wait()
        copy(jj, pos2_ref, buf2).wait()
        return carry

    lax.fori_loop(0, COMBINE_TM, start, 0)
    lax.fori_loop(0, COMBINE_TM, wait, 0)
    p = p_ref[...]
    y = p[:, 0:1] * buf1[...] + p[:, 1:2] * buf2[...]
    xn = x_ref[0] + gate_ref[0] * y
    o_ref[0] = xn * _rms_scale(xn) * fg_ref[...]


def _combine_call(pos, ys, x, p, gate, final_g):
    bsz, length, d_model = x.shape
    tm = COMBINE_TM
    nt = length // tm
    tokens = bsz * length
    row = pl.BlockSpec((1, tm, d_model), lambda b, i: (b, i, 0))
    vec = pl.BlockSpec((1, 1, d_model), lambda b, i: (b, 0, 0))
    ntok = tokens // tm
    return pl.pallas_call(
        _combine_kernel,
        grid=(bsz, nt),
        in_specs=[pl.BlockSpec((tm,), lambda b, i: (b * nt + i,), memory_space=pltpu.SMEM),
                  pl.BlockSpec((tm,), lambda b, i: (ntok + b * nt + i,), memory_space=pltpu.SMEM),
                  pl.BlockSpec(memory_space=pl.ANY), row,
                  pl.BlockSpec((tm, TOP_K), lambda b, i: (b * nt + i, 0)), vec,
                  _resident(final_g.shape)],
        out_specs=row,
        out_shape=jax.ShapeDtypeStruct(x.shape, F32),
        scratch_shapes=[pltpu.VMEM((tm, d_model), F32), pltpu.VMEM((tm, d_model), F32),
                        pltpu.SemaphoreType.DMA],
        compiler_params=_cparams(("arbitrary", "arbitrary")),
        name="moe_combine",
    )(pos, pos, ys, x, p, gate, final_g)


def _moe_plan(idx, tokens):
    e_flat = idx.T.reshape(-1)
    onehot = (e_flat[:, None] == jnp.arange(N_EXPERTS, dtype=jnp.int32)[None, :]).astype(jnp.int32)
    csum = jnp.cumsum(onehot, axis=0)
    rank = jnp.sum(csum * onehot, axis=1) - 1
    counts = csum[-1]
    tiles_e = (counts + MOE_TM - 1) // MOE_TM
    tile_end = jnp.cumsum(tiles_e)
    gstart = (tile_end - tiles_e) * MOE_TM
    pos = (jnp.sum(gstart[None, :] * onehot, axis=1) + rank).astype(jnp.int32)
    nt_max = (TOP_K * tokens) // MOE_TM + N_EXPERTS
    tile_gid = jnp.sum(jnp.arange(nt_max, dtype=jnp.int32)[:, None] >= tile_end[None, :], axis=1)
    tile_gid = jnp.minimum(tile_gid, N_EXPERTS - 1).astype(jnp.int32)
    return pos, tile_gid, tile_end[-1:].astype(jnp.int32), nt_max * MOE_TM


def kernel(x, c, ctx, c_ctx, w_ada, b_ada, norm1_g, w_in, conv_w, conv_b, dt_bias, a_log, d_skip,
           ssd_norm_g, w_ssd_br, gm_norm_g, w_spatial, b_spatial, w_gm_br, w_out, norm2_g, ffn_w_gate,
           ffn_w_up, ffn_w_down, moe_router_w, moe_router_b, moe_w_gate, moe_w_up, moe_w_down,
           final_norm_g):
    bsz, seq, d_model = x.shape
    ctx_len = ctx.shape[1]
    depth = w_in.shape[0]
    d_ssd = SSD_HEADS * SSD_HEAD_DIM
    d_xbc = d_ssd + 2 * SSD_GROUPS * SSD_STATE
    d_gm = d_model
    off_dt = d_ssd + d_xbc
    off_u = off_dt + 2 * SSD_HEADS

    rows = 16
    cc = jnp.concatenate([c, c_ctx[None, :], jnp.zeros((rows - bsz - 1, d_model), F32)], axis=0)
    mod = _mod_call(cc, w_ada, b_ada).reshape(depth, rows, N_MOD, d_model)

    head_of_col = jnp.arange(d_ssd, dtype=jnp.int32) // SSD_HEAD_DIM
    e_mat = (jnp.arange(LANES, dtype=jnp.int32)[:, None] == head_of_col[None, :]).astype(BF16)
    st_zero = jnp.zeros((bsz, SSD_GROUPS, SSD_STATE, GROUP_CH), F32)

    for i in range(depth):
        last = i == depth - 1
        mx = lambda k: mod[i, :bsz, k][:, None, :]
        mc = lambda k: jnp.broadcast_to(mod[i, bsz, k][None, None, :], (bsz, 1, d_model))

        wi = w_in[i]
        pad = jnp.zeros((d_model, LANES - SSD_HEADS), F32)
        w_dt = jnp.concatenate([wi[:, off_dt:off_dt + SSD_HEADS], pad,
                                wi[:, off_dt + SSD_HEADS:off_u], pad], axis=1)
        w_full = jnp.concatenate([wi[:, :off_dt], wi[:, off_u:], w_dt], axis=1).astype(BF16)
        zpad = jnp.zeros((LANES - SSD_HEADS,), F32)
        dtb_pad = jnp.concatenate([dt_bias[i, 0], zpad, dt_bias[i, 1], zpad])[None, :]
        alog_pad = jnp.pad(a_log[i], ((0, 0), (0, LANES - SSD_HEADS)))
        dskip_exp = jnp.repeat(d_skip[i], SSD_HEAD_DIM)[None, :]
        n1g = norm1_g[i][None, :]
        n2g = norm2_g[i][None, :]
        cw = conv_w[i]
        cb = conv_b[i][None, :]
        gmg = gm_norm_g[i][None, :]
        in_kw = dict(d_ssd=d_ssd, d_xbc=d_xbc, d_gm=d_gm)
        mix_w = (ssd_norm_g[i][None, :], w_ssd_br[i].astype(BF16), w_gm_br[i].astype(BF16),
                 w_out[i].astype(BF16), w_spatial[i].astype(BF16),
                 jnp.repeat(b_spatial[i].T, d_gm // GM_GROUPS, axis=1))

        if last:
            w_ssd_only = jnp.concatenate([wi[:, d_ssd:off_dt], w_dt], axis=1).astype(BF16)
            xbc_c, dt_c = _in_proj_call(ctx, mc(0), mc(1), n1g, w_ssd_only, cw, cb, dtb_pad, gmg,
                                        seg=ctx_len, tm=ctx_len, ssd_only=True, **in_kw)
            st_f, st_b = _ssd_call(xbc_c, dt_c, alog_pad, e_mat, dskip_exp, st_zero, st_zero,
                                   with_output=False, d_ssd=d_ssd)
        else:
            z_c, xbc_c, dt_c, u_c, v_c, gt_c = _in_proj_call(
                ctx, mc(0), mc(1), n1g, w_full, cw, cb, dtb_pad, gmg, seg=ctx_len, tm=ctx_len,
                ssd_only=False, **in_kw)
            yf_c, yb_c, st_f, st_b = _ssd_call(xbc_c, dt_c, alog_pad, e_mat, dskip_exp, st_zero,
                                               st_zero, with_output=True, d_ssd=d_ssd)
            ctx_mid = _merge_call(yf_c, yb_c, z_c, u_c, v_c, gt_c, ctx, mc(2), *mix_w, tm=ctx_len)

        z, xbc, dt, u, v, gt = _in_proj_call(x, mx(0), mx(1), n1g, w_full, cw, cb, dtb_pad, gmg,
                                             seg=GRID_W, tm=512, ssd_only=False, **in_kw)
        yf, yb, _, _ = _ssd_call(xbc, dt, alog_pad, e_mat, dskip_exp, st_f, st_b, with_output=True,
                                 d_ssd=d_ssd)
        x = _merge_call(yf, yb, z, u, v, gt, x, mx(2), *mix_w, tm=512)

        j = i // 2
        if i % 2 == 0:
            ffn_w = (ffn_w_gate[j].astype(BF16), ffn_w_up[j].astype(BF16), ffn_w_down[j].astype(BF16))
            x = _ffn_call(x, mx(3), mx(4), mx(5), n2g, *ffn_w, tm=512)
            if not last:
                ctx = _ffn_call(ctx_mid, mc(3), mc(4), mc(5), n2g, *ffn_w, tm=ctx_len)
        else:
            assert last, "the routed layer applies the final norm in its combine step"
            tokens = bsz * seq
            h2, idx, p = _router_call(x, mx(3), mx(4), n2g, moe_router_w[j].astype(BF16),
                                      moe_router_b[j][None, :], tm=512)
            pos, tile_gid, n_tiles, rows_sorted = _moe_plan(idx, tokens)
            xs = _dispatch_call(pos, h2, rows_sorted)
            ys = _experts_call(tile_gid, n_tiles, xs, moe_w_gate[j].astype(BF16),
                               moe_w_up[j].astype(BF16), moe_w_down[j].astype(BF16))
            x = _combine_call(pos, ys, x, p, mx(5), final_norm_g[None, :])
    return x
```

```python
import functools

import jax
import jax.numpy as jnp
from jax import lax
from jax.experimental import pallas as pl
from jax.experimental.pallas import tpu as pltpu

F32 = jnp.float32
BF16 = jnp.bfloat16
EPS = 1e-6

SSD_HEAD_DIM = 64
SSD_HEADS = 32
SSD_GROUPS = 4
SSD_STATE = 128
HEADS_PER_GROUP = SSD_HEADS // SSD_GROUPS
GROUP_CH = HEADS_PER_GROUP * SSD_HEAD_DIM
CONV_WIDTH = 5
GRID_W = 64
GM_CHUNK = 128
GM_GROUPS = 8
N_EXPERTS = 8
TOP_K = 2
N_MOD = 6

LANES = 128
SSD_Q = 128
NEG_BIG = -1e30

VMEM_LIMIT = 56 * 1024 * 1024


def _cparams(sem, vmem=VMEM_LIMIT):
    return pltpu.CompilerParams(dimension_semantics=sem, vmem_limit_bytes=vmem)


def _resident(shape):
    nd = len(shape)
    return pl.BlockSpec(shape, lambda *_: (0,) * nd, pipeline_mode=pl.Buffered(1))


def _silu(v):
    return v * jax.nn.sigmoid(v)


def _gelu_tanh(v):
    return 0.5 * v * (1.0 + jnp.tanh(0.7978845608028654 * (v + 0.044715 * v * v * v)))


def _softplus(v):
    return jnp.maximum(v, 0.0) + jnp.log1p(jnp.exp(-jnp.abs(v)))


def _rms_scale(v):
    return lax.rsqrt(jnp.mean(v * v, axis=-1, keepdims=True) + EPS)


def _dot(a, b):
    return jnp.dot(a, b, preferred_element_type=F32)


def _mod_kernel(cc_ref, w_ref, b_ref, o_ref):
    s = _silu(cc_ref[...])
    o_ref[0] = _dot(s.astype(BF16), w_ref[0].astype(BF16)) + b_ref[0]


def _mod_call(cc, w_ada, b_ada):
    depth, d, n = w_ada.shape
    tn = 1536
    rows = cc.shape[0]
    return pl.pallas_call(
        _mod_kernel,
        grid=(depth, n // tn),
        in_specs=[pl.BlockSpec((rows, d), lambda i, j: (0, 0)),
                  pl.BlockSpec((1, d, tn), lambda i, j: (i, 0, j)),
                  pl.BlockSpec((1, 1, tn), lambda i, j: (i, 0, j))],
        out_specs=pl.BlockSpec((1, rows, tn), lambda i, j: (i, 0, j)),
        out_shape=jax.ShapeDtypeStruct((depth, rows, n), F32),
        compiler_params=_cparams(("arbitrary", "arbitrary")),
        name="mod",
    )(cc, w_ada, b_ada.reshape(depth, 1, n))


IN_CHUNK = 512


def _in_proj_kernel(x_ref, sh_ref, sc_ref, g_ref, w_ref, cw_ref, cb_ref, dtb_ref, gmg_ref,
                    *out_refs, seg, ssd_only, d_ssd, d_xbc, d_gm, d_model):
    x = x_ref[0]
    tm = x.shape[0]
    h = x * _rms_scale(x) * g_ref[...]
    h = h * (1.0 + sc_ref[0]) + sh_ref[0]
    hb = h.astype(BF16)

    if ssd_only:
        xbc_ref, dt_ref = out_refs
        off_xbc = 0
    else:
        z_ref, xbc_ref, dt_ref, u_ref, v_ref, gt_ref = out_refs
        off_xbc = d_ssd
        for c in range(0, d_ssd, IN_CHUNK):
            acc = _dot(hb, w_ref[:, c:c + IN_CHUNK])
            z_ref[0, :, c:c + IN_CHUNK] = _silu(acc).astype(BF16)

    rpos = lax.broadcasted_iota(jnp.int32, (tm, 1), 0) & (seg - 1)
    for c in range(0, d_xbc, IN_CHUNK):
        cs = slice(c, c + IN_CHUNK)
        acc = _dot(hb, w_ref[:, off_xbc + c:off_xbc + c + IN_CHUNK])
        out = cb_ref[:, cs] + cw_ref[CONV_WIDTH // 2:CONV_WIDTH // 2 + 1, cs] * acc
        for k in range(CONV_WIDTH):
            d = k - CONV_WIDTH // 2
            if d == 0:
                continue
            shifted = pltpu.roll(acc, (-d) % tm, axis=0)
            valid = (rpos + d >= 0) & (rpos + d < seg)
            out = out + cw_ref[k:k + 1, cs] * jnp.where(valid, shifted, 0.0)
        xbc_ref[0, :, cs] = _silu(out).astype(BF16)

    off = off_xbc + d_xbc
    if not ssd_only:
        for c in range(0, d_gm, IN_CHUNK):
            acc = _dot(hb, w_ref[:, off + c:off + c + IN_CHUNK])
            u_ref[0, :, c:c + IN_CHUNK] = _gelu_tanh(acc).astype(BF16)
        off += d_gm
        gv = _gelu_tanh(_dot(hb, w_ref[:, off:off + d_gm]))
        v_ref[0] = (gv * _rms_scale(gv) * gmg_ref[...]).astype(BF16)
        off += d_gm
        for c in range(0, 2 * d_model, IN_CHUNK):
            acc = _dot(hb, w_ref[:, off + c:off + c + IN_CHUNK])
            gt_ref[0, :, c:c + IN_CHUNK] = jax.nn.sigmoid(acc).astype(BF16)
        off += 2 * d_model
    dt_ref[0] = _softplus(_dot(hb, w_ref[:, off:off + 2 * LANES]) + dtb_ref[...])


def _in_proj_call(x, shift, scale, n1g, w_perm, conv_w, conv_b, dtb_pad, gm_g, *, seg, tm, ssd_only,
                  d_ssd, d_xbc, d_gm):
    bsz, length, d_model = x.shape
    row = lambda width: pl.BlockSpec((1, tm, width), lambda b, i: (b, i, 0))
    vec = pl.BlockSpec((1, 1, d_model), lambda b, i: (b, 0, 0))
    shp = lambda width, dt: jax.ShapeDtypeStruct((bsz, length, width), dt)
    if ssd_only:
        out_specs = [row(d_xbc), row(2 * LANES)]
        out_shape = [shp(d_xbc, BF16), shp(2 * LANES, F32)]
    else:
        out_specs = [row(d_ssd), row(d_xbc), row(2 * LANES), row(d_gm), row(d_gm), row(2 * d_model)]
        out_shape = [shp(d_ssd, BF16), shp(d_xbc, BF16), shp(2 * LANES, F32), shp(d_gm, BF16),
                     shp(d_gm, BF16), shp(2 * d_model, BF16)]
    kern = functools.partial(_in_proj_kernel, seg=seg, ssd_only=ssd_only, d_ssd=d_ssd, d_xbc=d_xbc,
                             d_gm=d_gm, d_model=d_model)
    return pl.pallas_call(
        kern,
        grid=(bsz, length // tm),
        in_specs=[row(d_model), vec, vec, _resident(n1g.shape), _resident(w_perm.shape),
                  _resident(conv_w.shape), _resident(conv_b.shape), _resident(dtb_pad.shape),
                  _resident(gm_g.shape)],
        out_specs=out_specs,
        out_shape=out_shape,
        compiler_params=_cparams(("arbitrary", "arbitrary")),
        name="in_proj_ssd_only" if ssd_only else "in_proj",
    )(x, shift, scale, n1g, w_perm, conv_w, conv_b, dtb_pad, gm_g)


def _split3(v):
    v1 = v.astype(BF16)
    r1 = v - v1.astype(F32)
    v2 = r1.astype(BF16)
    v3 = (r1 - v2.astype(F32)).astype(BF16)
    return v1, v2, v3


def _ssd_direction(x_ref, b_ref, c_ref, dt_ref, aneg, e_ref, dskip_ref, s_ref, y_ref, *, reverse,
                   with_output):
    q = SSD_Q
    x = x_ref[0]
    dt = dt_ref[0]
    a = dt * aneg
    row = lax.broadcasted_iota(jnp.int32, (q, q), 0)
    col = lax.broadcasted_iota(jnp.int32, (q, q), 1)
    tri = (row <= col) if reverse else (row >= col)
    tmat = tri.astype(F32).astype(BF16)
    a1, a2, a3 = _split3(a)
    acs = _dot(tmat, a1) + _dot(tmat, a2) + _dot(tmat, a3)
    a_tot = acs[0:1] if reverse else acs[q - 1:q]
    eacs = jnp.exp(acs)
    w = jnp.exp(a_tot - acs) * dt
    eat = jnp.broadcast_to(jnp.exp(a_tot), (8, LANES))
    stack = jnp.concatenate([w, eacs, eat], axis=0)
    hi = stack.astype(BF16)
    lo = (stack - hi.astype(F32)).astype(BF16)
    expd = _dot(hi, e_ref[...]) + _dot(lo, e_ref[...])
    w_exp = expd[0:q]
    eacs_exp = expd[q:2 * q]
    eat_exp = expd[2 * q:2 * q + 1]

    xw = (x.astype(F32) * w_exp).astype(BF16)

    if with_output:
        acs_t = acs.T
        ldt_t = jnp.log(dt).T
        lane = lax.broadcasted_iota(jnp.int32, (q, LANES), 1)
        lo_half = lane < SSD_HEAD_DIM
        for g in range(SSD_GROUPS):
            cg = c_ref[0, :, g * SSD_STATE:(g + 1) * SSD_STATE]
            bg = b_ref[0, :, g * SSD_STATE:(g + 1) * SSD_STATE]
            cb = lax.dot_general(cg, bg, (((1,), (1,)), ((), ())), preferred_element_type=F32)
            sg = s_ref[g].astype(BF16)
            y_off = _dot(cg, sg) * eacs_exp[:, g * GROUP_CH:(g + 1) * GROUP_CH]
            for j in range(HEADS_PER_GROUP // 2):
                h0 = g * HEADS_PER_GROUP + 2 * j
                ms = []
                for hh in (h0, h0 + 1):
                    expo = acs[:, hh:hh + 1] + (ldt_t[hh:hh + 1, :] - acs_t[hh:hh + 1, :])
                    ms.append((cb * jnp.exp(jnp.where(tri, expo, NEG_BIG))).astype(BF16))
                lhs = jnp.concatenate(ms, axis=1)
                cs = slice(h0 * SSD_HEAD_DIM, (h0 + 2) * SSD_HEAD_DIM)
                xp = x[:, cs]
                zero = jnp.zeros_like(xp)
                rhs = jnp.concatenate([jnp.where(lo_half, xp, zero), jnp.where(lo_half, zero, xp)],
                                      axis=0)
                y = _dot(lhs, rhs) + y_off[:, 2 * j * SSD_HEAD_DIM:(2 * j + 2) * SSD_HEAD_DIM]
                if dskip_ref is not None:
                    y = y + xp.astype(F32) * dskip_ref[:, cs]
                y_ref[0, :, cs] = y.astype(y_ref.dtype)

    for g in range(SSD_GROUPS):
        bg = b_ref[0, :, g * SSD_STATE:(g + 1) * SSD_STATE]
        gs = slice(g * GROUP_CH, (g + 1) * GROUP_CH)
        new = lax.dot_general(bg, xw[:, gs], (((0,), (0,)), ((), ())), preferred_element_type=F32)
        s_ref[g] = s_ref[g] * eat_exp[:, gs] + new


def _ssd_kernel(xf_ref, bf_ref, cf_ref, dtf_ref, xb_ref, bb_ref, cb_ref, dtb_ref, alog_ref, e_ref,
                dskip_ref, h0f_ref, h0b_ref, *rest, with_output):
    if with_output:
        yf_ref, yb_ref, hff_ref, hfb_ref, sf_ref, sb_ref = rest
    else:
        hff_ref, hfb_ref, sf_ref, sb_ref = rest
        yf_ref = yb_ref = None
    c = pl.program_id(1)

    @pl.when(c == 0)
    def _():
        sf_ref[...] = h0f_ref[0]
        sb_ref[...] = h0b_ref[0]

    lane = lax.broadcasted_iota(jnp.int32, (1, LANES), 1)
    aneg = jnp.where(lane < SSD_HEADS, -jnp.exp(alog_ref[...]), 0.0)
    _ssd_direction(xf_ref, bf_ref, cf_ref, dtf_ref, aneg[0:1], e_ref, dskip_ref, sf_ref, yf_ref,
                   reverse=False, with_output=with_output)
    _ssd_direction(xb_ref, bb_ref, cb_ref, dtb_ref, aneg[1:2], e_ref, None, sb_ref, yb_ref,
                   reverse=True, with_output=with_output)

    @pl.when(c == pl.num_programs(1) - 1)
    def _():
        hff_ref[0] = sf_ref[...]
        hfb_ref[0] = sb_ref[...]


def _ssd_call(xbc, dt, alog_pad, e_mat, dskip_exp, h0f, h0b, *, with_output, d_ssd):
    bsz, length, d_xbc = xbc.shape
    q = SSD_Q
    nc = length // q
    nb = d_ssd // (SSD_GROUPS * SSD_STATE)
    fwd = lambda b, c: c
    bwd = lambda b, c: nc - 1 - c

    def chunk_specs(pos, dt_block):
        return [pl.BlockSpec((1, q, d_ssd), lambda b, c: (b, pos(b, c), 0)),
                pl.BlockSpec((1, q, SSD_GROUPS * SSD_STATE), lambda b, c: (b, pos(b, c), nb)),
                pl.BlockSpec((1, q, SSD_GROUPS * SSD_STATE), lambda b, c: (b, pos(b, c), nb + 1)),
                pl.BlockSpec((1, q, LANES), lambda b, c: (b, pos(b, c), dt_block))]

    st_shape = (SSD_GROUPS, SSD_STATE, GROUP_CH)
    st_spec = pl.BlockSpec((1,) + st_shape, lambda b, c: (b, 0, 0, 0))
    in_specs = (chunk_specs(fwd, 0) + chunk_specs(bwd, 1)
                + [_resident(alog_pad.shape), _resident(e_mat.shape), _resident(dskip_exp.shape),
                   st_spec, st_spec])
    st_out = jax.ShapeDtypeStruct((bsz,) + st_shape, F32)
    out_specs = [st_spec, st_spec]
    out_shape = [st_out, st_out]
    if with_output:
        y_out = jax.ShapeDtypeStruct((bsz, length, d_ssd), BF16)
        out_specs = [pl.BlockSpec((1, q, d_ssd), lambda b, c: (b, c, 0)),
                     pl.BlockSpec((1, q, d_ssd), lambda b, c: (b, nc - 1 - c, 0))] + out_specs
        out_shape = [y_out, y_out] + out_shape
    return pl.pallas_call(
        functools.partial(_ssd_kernel, with_output=with_output),
        grid=(bsz, nc),
        in_specs=in_specs,
        out_specs=out_specs,
        out_shape=out_shape,
        scratch_shapes=[pltpu.VMEM(st_shape, F32), pltpu.VMEM(st_shape, F32)],
        compiler_params=_cparams(("arbitrary", "arbitrary")),
        name="ssd" if with_output else "ssd_states",
    )(xbc, xbc, xbc, dt, xbc, xbc, xbc, dt, alog_pad, e_mat, dskip_exp, h0f, h0b)


def _merge_kernel(yf_ref, yb_ref, z_ref, u_ref, v_ref, gt_ref, x_ref, gate_ref, sg_ref, wssd_ref,
                  wgm_ref, wout_ref, wsp_ref, bsp_ref, o_ref, ygm_ref):
    tm = x_ref.shape[1]
    d_model = x_ref.shape[2]
    yz = (yf_ref[0].astype(F32) + yb_ref[0].astype(F32)) * z_ref[0].astype(F32)
    y_ssd = (yz * _rms_scale(yz) * sg_ref[...]).astype(BF16)
    br_ssd = _dot(y_ssd, wssd_ref[...])
    gch = d_model // GM_GROUPS
    for n in range(tm // GM_CHUNK):
        rs = slice(n * GM_CHUNK, (n + 1) * GM_CHUNK)
        for g in range(GM_GROUPS):
            cs = slice(g * gch, (g + 1) * gch)
            s = _dot(wsp_ref[g], v_ref[0, rs, cs]) + bsp_ref[:, cs]
            ygm_ref[rs, cs] = (u_ref[0, rs, cs].astype(F32) * s).astype(BF16)
    br_gm = _dot(ygm_ref[...], wgm_ref[...])
    merged = (gt_ref[0, :, :d_model].astype(F32) * br_ssd
              + gt_ref[0, :, d_model:].astype(F32) * br_gm).astype(BF16)
    o_ref[0] = x_ref[0] + gate_ref[0] * _dot(merged, wout_ref[...])


def _merge_call(yf, yb, z, u, v, gt, x, gate, ssd_g, w_ssd, w_gm, w_out, w_sp, bsp_exp, *, tm):
    bsz, length, d_model = x.shape
    row = lambda arr: pl.BlockSpec((1, tm, arr.shape[2]), lambda b, i: (b, i, 0))
    vec = pl.BlockSpec((1, 1, d_model), lambda b, i: (b, 0, 0))
    return pl.pallas_call(
        _merge_kernel,
        grid=(bsz, length // tm),
        in_specs=[row(yf), row(yb), row(z), row(u), row(v), row(gt), row(x), vec,
                  _resident(ssd_g.shape), _resident(w_ssd.shape), _resident(w_gm.shape),
                  _resident(w_out.shape), _resident(w_sp.shape), _resident(bsp_exp.shape)],
        out_specs=row(x),
        out_shape=jax.ShapeDtypeStruct(x.shape, F32),
        scratch_shapes=[pltpu.VMEM((tm, d_model), BF16)],
        compiler_params=_cparams(("arbitrary", "arbitrary")),
        name="merge",
    )(yf, yb, z, u, v, gt, x, gate, ssd_g, w_ssd, w_gm, w_out, w_sp, bsp_exp)


FFN_CHUNK = 256


def _ffn_kernel(x_ref, sh_ref, sc_ref, gate_ref, g_ref, wg_ref, wu_ref, wd_ref, o_ref):
    x = x_ref[0]
    h = x * _rms_scale(x) * g_ref[...]
    hb = (h * (1.0 + sc_ref[0]) + sh_ref[0]).astype(BF16)
    d_ff = wg_ref.shape[1]
    acc = jnp.zeros(x.shape, F32)
    step = 4 * FFN_CHUNK
    for c in range(0, d_ff, step):
        wdt = min(step, d_ff - c)
        act = (_silu(_dot(hb, wg_ref[:, c:c + wdt])) * _dot(hb, wu_ref[:, c:c + wdt])).astype(BF16)
        acc = acc + _dot(act, wd_ref[c:c + wdt, :])
    o_ref[0] = x + gate_ref[0] * acc


def _ffn_call(x, shift, scale, gate, n2g, w_gate, w_up, w_down, *, tm):
    bsz, length, d_model = x.shape
    row = pl.BlockSpec((1, tm, d_model), lambda b, i: (b, i, 0))
    vec = pl.BlockSpec((1, 1, d_model), lambda b, i: (b, 0, 0))
    return pl.pallas_call(
        _ffn_kernel,
        grid=(bsz, length // tm),
        in_specs=[row, vec, vec, vec, _resident(n2g.shape), _resident(w_gate.shape),
                  _resident(w_up.shape), _resident(w_down.shape)],
        out_specs=row,
        out_shape=jax.ShapeDtypeStruct(x.shape, F32),
        compiler_params=_cparams(("arbitrary", "arbitrary")),
        name="ffn",
    )(x, shift, scale, gate, n2g, w_gate, w_up, w_down)


def _router_kernel(x_ref, sh_ref, sc_ref, g_ref, wr_ref, br_ref, h_ref, idx_ref, p_ref):
    x = x_ref[0]
    h = x * _rms_scale(x) * g_ref[...]
    h = h * (1.0 + sc_ref[0]) + sh_ref[0]
    h_ref[...] = h
    logits = _dot(h.astype(BF16), wr_ref[...]) + br_ref[...]
    eidx = lax.broadcasted_iota(jnp.int32, logits.shape, 1).astype(F32)
    no_expert = float(N_EXPERTS)
    m1 = jnp.max(logits, axis=-1, keepdims=True)
    i1 = jnp.min(jnp.where(logits == m1, eidx, no_expert), axis=-1, keepdims=True)
    rest = jnp.where(eidx == i1, -jnp.inf, logits)
    m2 = jnp.max(rest, axis=-1, keepdims=True)
    i2 = jnp.min(jnp.where(rest == m2, eidx, no_expert), axis=-1, keepdims=True)
    e2 = jnp.exp(m2 - m1)
    p1 = 1.0 / (1.0 + e2)
    first = lax.broadcasted_iota(jnp.int32, idx_ref.shape, 1) == 0
    idx_ref[...] = jnp.where(first, i1, i2).astype(jnp.int32)
    p_ref[...] = jnp.where(first, p1, e2 * p1)


def _router_call(x, shift, scale, n2g, w_router, b_router, *, tm):
    bsz, length, d_model = x.shape
    nt = length // tm
    tokens = bsz * length
    row = pl.BlockSpec((1, tm, d_model), lambda b, i: (b, i, 0))
    vec = pl.BlockSpec((1, 1, d_model), lambda b, i: (b, 0, 0))
    flat = lambda width: pl.BlockSpec((tm, width), lambda b, i: (b * nt + i, 0))
    return pl.pallas_call(
        _router_kernel,
        grid=(bsz, nt),
        in_specs=[row, vec, vec, _resident(n2g.shape), _resident(w_router.shape),
                  _resident(b_router.shape)],
        out_specs=[flat(d_model), flat(TOP_K), flat(TOP_K)],
        out_shape=[jax.ShapeDtypeStruct((tokens, d_model), F32),
                   jax.ShapeDtypeStruct((tokens, TOP_K), jnp.int32),
                   jax.ShapeDtypeStruct((tokens, TOP_K), F32)],
        compiler_params=_cparams(("arbitrary", "arbitrary")),
        name="router",
    )(x, shift, scale, n2g, w_router, b_router)


DISPATCH_TM = 512


def _dispatch_kernel(pos1_ref, pos2_ref, h_ref, init_hbm, xs_hbm, sem):
    del init_hbm

    def copy(jj, pos_ref):
        return pltpu.make_async_copy(h_ref.at[pl.ds(jj, 1)], xs_hbm.at[pl.ds(pos_ref[jj], 1)], sem)

    def start(jj, carry):
        copy(jj, pos1_ref).start()
        copy(jj, pos2_ref).start()
        return carry

    def wait(jj, carry):
        copy(jj, pos1_ref).wait()
        copy(jj, pos2_ref).wait()
        return carry

    lax.fori_loop(0, DISPATCH_TM, start, 0)
    lax.fori_loop(0, DISPATCH_TM, wait, 0)


def _dispatch_call(pos, h, rows):
    tokens, d_model = h.shape
    tm = DISPATCH_TM
    nt = tokens // tm
    init = jnp.zeros((rows, d_model), F32)
    return pl.pallas_call(
        _dispatch_kernel,
        grid=(nt,),
        in_specs=[pl.BlockSpec((tm,), lambda i: (i,), memory_space=pltpu.SMEM),
                  pl.BlockSpec((tm,), lambda i: (nt + i,), memory_space=pltpu.SMEM),
                  pl.BlockSpec((tm, d_model), lambda i: (i, 0)),
                  pl.BlockSpec(memory_space=pl.ANY)],
        out_specs=pl.BlockSpec(memory_space=pl.ANY),
        out_shape=jax.ShapeDtypeStruct((rows, d_model), F32),
        scratch_shapes=[pltpu.SemaphoreType.DMA],
        input_output_aliases={3: 0},
        compiler_params=_cparams(("arbitrary",)),
        name="moe_dispatch",
    )(pos, pos, h, init)


MOE_TM = 512
MOE_TF = 512


def _experts_kernel(gid_ref, nt_ref, xs_ref, wg_ref, wu_ref, wd_ref, ys_ref, xb_ref, acc_ref):
    m = pl.program_id(0)
    f = pl.program_id(1)

    @pl.when(m < nt_ref[0])
    def _():
        @pl.when(f == 0)
        def _():
            xb_ref[...] = xs_ref[...].astype(BF16)

        xb = xb_ref[...]
        act = (_silu(_dot(xb, wg_ref[0])) * _dot(xb, wu_ref[0])).astype(BF16)
        part = _dot(act, wd_ref[0])

        @pl.when(f == 0)
        def _():
            acc_ref[...] = part

        @pl.when(f > 0)
        def _():
            acc_ref[...] += part

        @pl.when(f == pl.num_programs(1) - 1)
        def _():
            ys_ref[...] = acc_ref[...]

    @pl.when(m >= nt_ref[0])
    def _():
        ys_ref[...] = jnp.zeros_like(ys_ref)


def _experts_call(tile_gid, n_tiles, xs, w_gate, w_up, w_down):
    rows, d_model = xs.shape
    d_ff = w_gate.shape[2]
    nt_max = rows // MOE_TM
    nf = d_ff // MOE_TF

    def tile(m, nt):
        return jnp.minimum(m, nt[0] - 1)

    def fcol(m, f, nt):
        return jnp.where(m < nt[0], f, nf - 1)

    grid_spec = pltpu.PrefetchScalarGridSpec(
        num_scalar_prefetch=2,
        grid=(nt_max, nf),
        in_specs=[pl.BlockSpec((MOE_TM, d_model), lambda m, f, gid, nt: (tile(m, nt), 0)),
                  pl.BlockSpec((1, d_model, MOE_TF),
                               lambda m, f, gid, nt: (gid[tile(m, nt)], 0, fcol(m, f, nt))),
                  pl.BlockSpec((1, d_model, MOE_TF),
                               lambda m, f, gid, nt: (gid[tile(m, nt)], 0, fcol(m, f, nt))),
                  pl.BlockSpec((1, MOE_TF, d_model),
                               lambda m, f, gid, nt: (gid[tile(m, nt)], fcol(m, f, nt), 0))],
        out_specs=pl.BlockSpec((MOE_TM, d_model), lambda m, f, gid, nt: (m, 0)),
        scratch_shapes=[pltpu.VMEM((MOE_TM, d_model), BF16), pltpu.VMEM((MOE_TM, d_model), F32)],
    )
    return pl.pallas_call(
        _experts_kernel,
        grid_spec=grid_spec,
        out_shape=jax.ShapeDtypeStruct((rows, d_model), F32),
        compiler_params=_cparams(("arbitrary", "arbitrary")),
        name="moe_experts",
    )(tile_gid, n_tiles, xs, w_gate, w_up, w_down)


COMBINE_TM = 256


def _combine_kernel(pos1_ref, pos2_ref, ys_hbm, x_ref, p_ref, gate_ref, fg_ref, o_ref, buf1, buf2, sem):
    def copy(jj, pos_ref, buf):
        return pltpu.make_async_copy(ys_hbm.at[pl.ds(pos_ref[jj], 1)], buf.at[pl.ds(jj, 1)], sem)

    def start(jj, carry):
        copy(jj, pos1_ref, buf1).start()
        copy(jj, pos2_ref, buf2).start()
        return carry

    def wait(jj, carry):
        copy(jj, pos1_ref, buf1).wait()
        copy(jj, pos2_ref, buf2).wait()
        return carry

    lax.fori_loop(0, COMBINE_TM, start, 0)
    lax.fori_loop(0, COMBINE_TM, wait, 0)
    p = p_ref[...]
    y = p[:, 0:1] * buf1[...] + p[:, 1:2] * buf2[...]
    xn = x_ref[0] + gate_ref[0] * y
    o_ref[0] = xn * _rms_scale(xn) * fg_ref[...]


def _combine_call(pos, ys, x, p, gate, final_g):
    bsz, length, d_model = x.shape
    tm = COMBINE_TM
    nt = length // tm
    tokens = bsz * length
    row = pl.BlockSpec((1, tm, d_model), lambda b, i: (b, i, 0))
    vec = pl.BlockSpec((1, 1, d_model), lambda b, i: (b, 0, 0))
    ntok = tokens // tm
    return pl.pallas_call(
        _combine_kernel,
        grid=(bsz, nt),
        in_specs=[pl.BlockSpec((tm,), lambda b, i: (b * nt + i,), memory_space=pltpu.SMEM),
                  pl.BlockSpec((tm,), lambda b, i: (ntok + b * nt + i,), memory_space=pltpu.SMEM),
                  pl.BlockSpec(memory_space=pl.ANY), row,
                  pl.BlockSpec((tm, TOP_K), lambda b, i: (b * nt + i, 0)), vec,
                  _resident(final_g.shape)],
        out_specs=row,
        out_shape=jax.ShapeDtypeStruct(x.shape, F32),
        scratch_shapes=[pltpu.VMEM((tm, d_model), F32), pltpu.VMEM((tm, d_model), F32),
                        pltpu.SemaphoreType.DMA],
        compiler_params=_cparams(("arbitrary", "arbitrary")),
        name="moe_combine",
    )(pos, pos, ys, x, p, gate, final_g)


def _moe_plan(idx, tokens):
    e_flat = idx.T.reshape(-1)
    onehot = (e_flat[:, None] == jnp.arange(N_EXPERTS, dtype=jnp.int32)[None, :]).astype(jnp.int32)
    csum = jnp.cumsum(onehot, axis=0)
    rank = jnp.sum(csum * onehot, axis=1) - 1
    counts = csum[-1]
    tiles_e = (counts + MOE_TM - 1) // MOE_TM
    tile_end = jnp.cumsum(tiles_e)
    gstart = (tile_end - tiles_e) * MOE_TM
    pos = (jnp.sum(gstart[None, :] * onehot, axis=1) + rank).astype(jnp.int32)
    nt_max = (TOP_K * tokens) // MOE_TM + N_EXPERTS
    tile_gid = jnp.sum(jnp.arange(nt_max, dtype=jnp.int32)[:, None] >= tile_end[None, :], axis=1)
    tile_gid = jnp.minimum(tile_gid, N_EXPERTS - 1).astype(jnp.int32)
    return pos, tile_gid, tile_end[-1:].astype(jnp.int32), nt_max * MOE_TM


def kernel(x, c, ctx, c_ctx, w_ada, b_ada, norm1_g, w_in, conv_w, conv_b, dt_bias, a_log, d_skip,
           ssd_norm_g, w_ssd_br, gm_norm_g, w_spatial, b_spatial, w_gm_br, w_out, norm2_g, ffn_w_gate,
           ffn_w_up, ffn_w_down, moe_router_w, moe_router_b, moe_w_gate, moe_w_up, moe_w_down,
           final_norm_g):
    bsz, seq, d_model = x.shape
    ctx_len = ctx.shape[1]
    depth = w_in.shape[0]
    d_ssd = SSD_HEADS * SSD_HEAD_DIM
    d_xbc = d_ssd + 2 * SSD_GROUPS * SSD_STATE
    d_gm = d_model
    off_dt = d_ssd + d_xbc
    off_u = off_dt + 2 * SSD_HEADS

    rows = 16
    cc = jnp.concatenate([c, c_ctx[None, :], jnp.zeros((rows - bsz - 1, d_model), F32)], axis=0)
    mod = _mod_call(cc, w_ada, b_ada).reshape(depth, rows, N_MOD, d_model)

    head_of_col = jnp.arange(d_ssd, dtype=jnp.int32) // SSD_HEAD_DIM
    e_mat = (jnp.arange(LANES, dtype=jnp.int32)[:, None] == head_of_col[None, :]).astype(BF16)
    st_zero = jnp.zeros((bsz, SSD_GROUPS, SSD_STATE, GROUP_CH), F32)

    for i in range(depth):
        last = i == depth - 1
        mx = lambda k: mod[i, :bsz, k][:, None, :]
        mc = lambda k: jnp.broadcast_to(mod[i, bsz, k][None, None, :], (bsz, 1, d_model))

        wi = w_in[i]
        pad = jnp.zeros((d_model, LANES - SSD_HEADS), F32)
        w_dt = jnp.concatenate([wi[:, off_dt:off_dt + SSD_HEADS], pad,
                                wi[:, off_dt + SSD_HEADS:off_u], pad], axis=1)
        w_full = jnp.concatenate([wi[:, :off_dt], wi[:, off_u:], w_dt], axis=1).astype(BF16)
        zpad = jnp.zeros((LANES - SSD_HEADS,), F32)
        dtb_pad = jnp.concatenate([dt_bias[i, 0], zpad, dt_bias[i, 1], zpad])[None, :]
        alog_pad = jnp.pad(a_log[i], ((0, 0), (0, LANES - SSD_HEADS)))
        dskip_exp = jnp.repeat(d_skip[i], SSD_HEAD_DIM)[None, :]
        n1g = norm1_g[i][None, :]
        n2g = norm2_g[i][None, :]
        cw = conv_w[i]
        cb = conv_b[i][None, :]
        gmg = gm_norm_g[i][None, :]
        in_kw = dict(d_ssd=d_ssd, d_xbc=d_xbc, d_gm=d_gm)
        mix_w = (ssd_norm_g[i][None, :], w_ssd_br[i].astype(BF16), w_gm_br[i].astype(BF16),
                 w_out[i].astype(BF16), w_spatial[i].astype(BF16),
                 jnp.repeat(b_spatial[i].T, d_gm // GM_GROUPS, axis=1))

        if last:
            w_ssd_only = jnp.concatenate([wi[:, d_ssd:off_dt], w_dt], axis=1).astype(BF16)
            xbc_c, dt_c = _in_proj_call(ctx, mc(0), mc(1), n1g, w_ssd_only, cw, cb, dtb_pad, gmg,
                                        seg=ctx_len, tm=ctx_len, ssd_only=True, **in_kw)
            st_f, st_b = _ssd_call(xbc_c, dt_c, alog_pad, e_mat, dskip_exp, st_zero, st_zero,
                                   with_output=False, d_ssd=d_ssd)
        else:
            z_c, xbc_c, dt_c, u_c, v_c, gt_c = _in_proj_call(
                ctx, mc(0), mc(1), n1g, w_full, cw, cb, dtb_pad, gmg, seg=ctx_len, tm=ctx_len,
                ssd_only=False, **in_kw)
            yf_c, yb_c, st_f, st_b = _ssd_call(xbc_c, dt_c, alog_pad, e_mat, dskip_exp, st_zero,
                                               st_zero, with_output=True, d_ssd=d_ssd)
            ctx_mid = _merge_call(yf_c, yb_c, z_c, u_c, v_c, gt_c, ctx, mc(2), *mix_w, tm=ctx_len)

        z, xbc, dt, u, v, gt = _in_proj_call(x, mx(0), mx(1), n1g, w_full, cw, cb, dtb_pad, gmg,
                                             seg=GRID_W, tm=512, ssd_only=False, **in_kw)
        yf, yb, _, _ = _ssd_call(xbc, dt, alog_pad, e_mat, dskip_exp, st_f, st_b, with_output=True,
                                 d_ssd=d_ssd)
        x = _merge_call(yf, yb, z, u, v, gt, x, mx(2), *mix_w, tm=512)

        j = i // 2
        if i % 2 == 0:
            ffn_w = (ffn_w_gate[j].astype(BF16), ffn_w_up[j].astype(BF16), ffn_w_down[j].astype(BF16))
            x = _ffn_call(x, mx(3), mx(4), mx(5), n2g, *ffn_w, tm=512)
            if not last:
                ctx = _ffn_call(ctx_mid, mc(3), mc(4), mc(5), n2g, *ffn_w, tm=ctx_len)
        else:
            assert last, "the routed layer applies the final norm in its combine step"
            tokens = bsz * seq
            h2, idx, p = _router_call(x, mx(3), mx(4), n2g, moe_router_w[j].astype(BF16),
                                      moe_router_b[j][None, :], tm=512)
            pos, tile_gid, n_tiles, rows_sorted = _moe_plan(idx, tokens)
            xs = _dispatch_call(pos, h2, rows_sorted)
            ys = _experts_call(tile_gid, n_tiles, xs, moe_w_gate[j].astype(BF16),
                               moe_w_up[j].astype(BF16), moe_w_down[j].astype(BF16))
            x = _combine_call(pos, ys, x, p, mx(5), final_norm_g[None, :])
    return x
```

```python
import functools

import jax
import jax.numpy as jnp
from jax import lax
from jax.experimental import pallas as pl
from jax.experimental.pallas import tpu as pltpu

F32 = jnp.float32
BF16 = jnp.bfloat16
EPS = 1e-6

SSD_HEAD_DIM = 64
SSD_HEADS = 32
SSD_GROUPS = 4
SSD_STATE = 128
HEADS_PER_GROUP = SSD_HEADS // SSD_GROUPS
GROUP_CH = HEADS_PER_GROUP * SSD_HEAD_DIM
CONV_WIDTH = 5
GRID_W = 64
GM_CHUNK = 128
GM_GROUPS = 8
N_EXPERTS = 8
TOP_K = 2
N_MOD = 6

LANES = 128
SSD_Q = 128
NEG_BIG = -1e30

VMEM_LIMIT = 56 * 1024 * 1024


def _cparams(sem, vmem=VMEM_LIMIT):
    return pltpu.CompilerParams(dimension_semantics=sem, vmem_limit_bytes=vmem)


def _resident(shape):
    nd = len(shape)
    return pl.BlockSpec(shape, lambda *_: (0,) * nd, pipeline_mode=pl.Buffered(1))


def _silu(v):
    return v * jax.nn.sigmoid(v)


def _gelu_tanh(v):
    return 0.5 * v * (1.0 + jnp.tanh(0.7978845608028654 * (v + 0.044715 * v * v * v)))


def _softplus(v):
    return jnp.maximum(v, 0.0) + jnp.log1p(jnp.exp(-jnp.abs(v)))


def _rms_scale(v):
    return lax.rsqrt(jnp.mean(v * v, axis=-1, keepdims=True) + EPS)


def _dot(a, b):
    return jnp.dot(a, b, preferred_element_type=F32)


def _mod_kernel(cc_ref, w_ref, b_ref, o_ref):
    s = _silu(cc_ref[...])
    o_ref[0] = _dot(s.astype(BF16), w_ref[0].astype(BF16)) + b_ref[0]


def _mod_call(cc, w_ada, b_ada):
    depth, d, n = w_ada.shape
    tn = 1536
    rows = cc.shape[0]
    return pl.pallas_call(
        _mod_kernel,
        grid=(depth, n // tn),
        in_specs=[pl.BlockSpec((rows, d), lambda i, j: (0, 0)),
                  pl.BlockSpec((1, d, tn), lambda i, j: (i, 0, j)),
                  pl.BlockSpec((1, 1, tn), lambda i, j: (i, 0, j))],
        out_specs=pl.BlockSpec((1, rows, tn), lambda i, j: (i, 0, j)),
        out_shape=jax.ShapeDtypeStruct((depth, rows, n), F32),
        compiler_params=_cparams(("arbitrary", "arbitrary")),
        name="mod",
    )(cc, w_ada, b_ada.reshape(depth, 1, n))


IN_CHUNK = 512


def _in_proj_kernel(x_ref, sh_ref, sc_ref, g_ref, w_ref, cw_ref, cb_ref, dtb_ref, gmg_ref,
                    *out_refs, seg, ssd_only, d_ssd, d_xbc, d_gm, d_model):
    x = x_ref[0]
    tm = x.shape[0]
    h = x * _rms_scale(x) * g_ref[...]
    h = h * (1.0 + sc_ref[0]) + sh_ref[0]
    hb = h.astype(BF16)

    if ssd_only:
        xbc_ref, dt_ref = out_refs
        off_xbc = 0
    else:
        z_ref, xbc_ref, dt_ref, u_ref, v_ref, gt_ref = out_refs
        off_xbc = d_ssd
        for c in range(0, d_ssd, IN_CHUNK):
            acc = _dot(hb, w_ref[:, c:c + IN_CHUNK])
            z_ref[0, :, c:c + IN_CHUNK] = _silu(acc).astype(BF16)

    rpos = lax.broadcasted_iota(jnp.int32, (tm, 1), 0) & (seg - 1)
    for c in range(0, d_xbc, IN_CHUNK):
        cs = slice(c, c + IN_CHUNK)
        acc = _dot(hb, w_ref[:, off_xbc + c:off_xbc + c + IN_CHUNK])
        out = cb_ref[:, cs] + cw_ref[CONV_WIDTH // 2:CONV_WIDTH // 2 + 1, cs] * acc
        for k in range(CONV_WIDTH):
            d = k - CONV_WIDTH // 2
            if d == 0:
                continue
            shifted = pltpu.roll(acc, (-d) % tm, axis=0)
            valid = (rpos + d >= 0) & (rpos + d < seg)
            out = out + cw_ref[k:k + 1, cs] * jnp.where(valid, shifted, 0.0)
        xbc_ref[0, :, cs] = _silu(out).astype(BF16)

    off = off_xbc + d_xbc
    if not ssd_only:
        for c in range(0, d_gm, IN_CHUNK):
            acc = _dot(hb, w_ref[:, off + c:off + c + IN_CHUNK])
            u_ref[0, :, c:c + IN_CHUNK] = _gelu_tanh(acc).astype(BF16)
        off += d_gm
        gv = _gelu_tanh(_dot(hb, w_ref[:, off:off + d_gm]))
        v_ref[0] = (gv * _rms_scale(gv) * gmg_ref[...]).astype(BF16)
        off += d_gm
        for c in range(0, 2 * d_model, IN_CHUNK):
            acc = _dot(hb, w_ref[:, off + c:off + c + IN_CHUNK])
            gt_ref[0, :, c:c + IN_CHUNK] = jax.nn.sigmoid(acc).astype(BF16)
        off += 2 * d_model
    dt_ref[0] = _softplus(_dot(hb, w_ref[:, off:off + 2 * LANES]) + dtb_ref[...])


def _in_proj_call(x, shift, scale, n1g, w_perm, conv_w, conv_b, dtb_pad, gm_g, *, seg, tm, ssd_only,
                  d_ssd, d_xbc, d_gm):
    bsz, length, d_model = x.shape
    row = lambda width: pl.BlockSpec((1, tm, width), lambda b, i: (b, i, 0))
    vec = pl.BlockSpec((1, 1, d_model), lambda b, i: (b, 0, 0))
    shp = lambda width, dt: jax.ShapeDtypeStruct((bsz, length, width), dt)
    if ssd_only:
        out_specs = [row(d_xbc), row(2 * LANES)]
        out_shape = [shp(d_xbc, BF16), shp(2 * LANES, F32)]
    else:
        out_specs = [row(d_ssd), row(d_xbc), row(2 * LANES), row(d_gm), row(d_gm), row(2 * d_model)]
        out_shape = [shp(d_ssd, BF16), shp(d_xbc, BF16), shp(2 * LANES, F32), shp(d_gm, BF16),
                     shp(d_gm, BF16), shp(2 * d_model, BF16)]
    kern = functools.partial(_in_proj_kernel, seg=seg, ssd_only=ssd_only, d_ssd=d_ssd, d_xbc=d_xbc,
                             d_gm=d_gm, d_model=d_model)
    return pl.pallas_call(
        kern,
        grid=(bsz, length // tm),
        in_specs=[row(d_model), vec, vec, _resident(n1g.shape), _resident(w_perm.shape),
                  _resident(conv_w.shape), _resident(conv_b.shape), _resident(dtb_pad.shape),
                  _resident(gm_g.shape)],
        out_specs=out_specs,
        out_shape=out_shape,
        compiler_params=_cparams(("arbitrary", "arbitrary")),
        name="in_proj_ssd_only" if ssd_only else "in_proj",
    )(x, shift, scale, n1g, w_perm, conv_w, conv_b, dtb_pad, gm_g)


def _split3(v):
    v1 = v.astype(BF16)
    r1 = v - v1.astype(F32)
    v2 = r1.astype(BF16)
    v3 = (r1 - v2.astype(F32)).astype(BF16)
    return v1, v2, v3


def _ssd_direction(x_ref, b_ref, c_ref, dt_ref, aneg, e_ref, dskip_ref, s_ref, y_ref, *, reverse,
                   with_output):
    q = SSD_Q
    x = x_ref[0]
    dt = dt_ref[0]
    a = dt * aneg
    row = lax.broadcasted_iota(jnp.int32, (q, q), 0)
    col = lax.broadcasted_iota(jnp.int32, (q, q), 1)
    tri = (row <= col) if reverse else (row >= col)
    tmat = tri.astype(F32).astype(BF16)
    a1, a2, a3 = _split3(a)
    acs = _dot(tmat, a1) + _dot(tmat, a2) + _dot(tmat, a3)
    a_tot = acs[0:1] if reverse else acs[q - 1:q]
    eacs = jnp.exp(acs)
    w = jnp.exp(a_tot - acs) * dt
    eat = jnp.broadcast_to(jnp.exp(a_tot), (8, LANES))
    stack = jnp.concatenate([w, eacs, eat], axis=0)
    expd = _dot(stack.astype(BF16), e_ref[...])
    w_exp = expd[0:q]
    eacs_exp = expd[q:2 * q]
    eat_exp = expd[2 * q:2 * q + 1]

    xw = (x.astype(F32) * w_exp).astype(BF16)

    if with_output:
        acs_t = acs.T
        ldt_t = jnp.log(dt).T
        lane = lax.broadcasted_iota(jnp.int32, (q, LANES), 1)
        lo_half = lane < SSD_HEAD_DIM
        for g in range(SSD_GROUPS):
            cg = c_ref[0, :, g * SSD_STATE:(g + 1) * SSD_STATE]
            bg = b_ref[0, :, g * SSD_STATE:(g + 1) * SSD_STATE]
            cb = lax.dot_general(cg, bg, (((1,), (1,)), ((), ())), preferred_element_type=F32)
            sg = s_ref[g].astype(BF16)
            y_off = _dot(cg, sg) * eacs_exp[:, g * GROUP_CH:(g + 1) * GROUP_CH]
            for j in range(HEADS_PER_GROUP // 2):
                h0 = g * HEADS_PER_GROUP + 2 * j
                ms = []
                for hh in (h0, h0 + 1):
                    expo = acs[:, hh:hh + 1] + (ldt_t[hh:hh + 1, :] - acs_t[hh:hh + 1, :])
                    ms.append((cb * jnp.exp(jnp.where(tri, expo, NEG_BIG))).astype(BF16))
                lhs = jnp.concatenate(ms, axis=1)
                cs = slice(h0 * SSD_HEAD_DIM, (h0 + 2) * SSD_HEAD_DIM)
                xp = x[:, cs]
                zero = jnp.zeros_like(xp)
                rhs = jnp.concatenate([jnp.where(lo_half, xp, zero), jnp.where(lo_half, zero, xp)],
                                      axis=0)
                y = _dot(lhs, rhs) + y_off[:, 2 * j * SSD_HEAD_DIM:(2 * j + 2) * SSD_HEAD_DIM]
                if dskip_ref is not None:
                    y = y + xp.astype(F32) * dskip_ref[:, cs]
                y_ref[0, :, cs] = y.astype(y_ref.dtype)

    for g in range(SSD_GROUPS):
        bg = b_ref[0, :, g * SSD_STATE:(g + 1) * SSD_STATE]
        gs = slice(g * GROUP_CH, (g + 1) * GROUP_CH)
        new = lax.dot_general(bg, xw[:, gs], (((0,), (0,)), ((), ())), preferred_element_type=F32)
        s_ref[g] = s_ref[g] * eat_exp[:, gs] + new


def _ssd_kernel(xf_ref, bf_ref, cf_ref, dtf_ref, xb_ref, bb_ref, cb_ref, dtb_ref, alog_ref, e_ref,
                dskip_ref, h0f_ref, h0b_ref, *rest, with_output):
    if with_output:
        yf_ref, yb_ref, hff_ref, hfb_ref, sf_ref, sb_ref = rest
    else:
        hff_ref, hfb_ref, sf_ref, sb_ref = rest
        yf_ref = yb_ref = None
    c = pl.program_id(1)

    @pl.when(c == 0)
    def _():
        sf_ref[...] = h0f_ref[0]
        sb_ref[...] = h0b_ref[0]

    lane = lax.broadcasted_iota(jnp.int32, (1, LANES), 1)
    aneg = jnp.where(lane < SSD_HEADS, -jnp.exp(alog_ref[...]), 0.0)
    _ssd_direction(xf_ref, bf_ref, cf_ref, dtf_ref, aneg[0:1], e_ref, dskip_ref, sf_ref, yf_ref,
                   reverse=False, with_output=with_output)
    _ssd_direction(xb_ref, bb_ref, cb_ref, dtb_ref, aneg[1:2], e_ref, None, sb_ref, yb_ref,
                   reverse=True, with_output=with_output)

    @pl.when(c == pl.num_programs(1) - 1)
    def _():
        hff_ref[0] = sf_ref[...]
        hfb_ref[0] = sb_ref[...]


def _ssd_call(xbc, dt, alog_pad, e_mat, dskip_exp, h0f, h0b, *, with_output, d_ssd):
    bsz, length, d_xbc = xbc.shape
    q = SSD_Q
    nc = length // q
    nb = d_ssd // (SSD_GROUPS * SSD_STATE)
    fwd = lambda b, c: c
    bwd = lambda b, c: nc - 1 - c

    def chunk_specs(pos, dt_block):
        return [pl.BlockSpec((1, q, d_ssd), lambda b, c: (b, pos(b, c), 0)),
                pl.BlockSpec((1, q, SSD_GROUPS * SSD_STATE), lambda b, c: (b, pos(b, c), nb)),
                pl.BlockSpec((1, q, SSD_GROUPS * SSD_STATE), lambda b, c: (b, pos(b, c), nb + 1)),
                pl.BlockSpec((1, q, LANES), lambda b, c: (b, pos(b, c), dt_block))]

    st_shape = (SSD_GROUPS, SSD_STATE, GROUP_CH)
    st_spec = pl.BlockSpec((1,) + st_shape, lambda b, c: (b, 0, 0, 0))
    in_specs = (chunk_specs(fwd, 0) + chunk_specs(bwd, 1)
                + [_resident(alog_pad.shape), _resident(e_mat.shape), _resident(dskip_exp.shape),
                   st_spec, st_spec])
    st_out = jax.ShapeDtypeStruct((bsz,) + st_shape, F32)
    out_specs = [st_spec, st_spec]
    out_shape = [st_out, st_out]
    if with_output:
        y_out = jax.ShapeDtypeStruct((bsz, length, d_ssd), BF16)
        out_specs = [pl.BlockSpec((1, q, d_ssd), lambda b, c: (b, c, 0)),
                     pl.BlockSpec((1, q, d_ssd), lambda b, c: (b, nc - 1 - c, 0))] + out_specs
        out_shape = [y_out, y_out] + out_shape
    return pl.pallas_call(
        functools.partial(_ssd_kernel, with_output=with_output),
        grid=(bsz, nc),
        in_specs=in_specs,
        out_specs=out_specs,
        out_shape=out_shape,
        scratch_shapes=[pltpu.VMEM(st_shape, F32), pltpu.VMEM(st_shape, F32)],
        compiler_params=_cparams(("arbitrary", "arbitrary")),
        name="ssd" if with_output else "ssd_states",
    )(xbc, xbc, xbc, dt, xbc, xbc, xbc, dt, alog_pad, e_mat, dskip_exp, h0f, h0b)


def _merge_kernel(yf_ref, yb_ref, z_ref, u_ref, v_ref, gt_ref, x_ref, gate_ref, sg_ref, wssd_ref,
                  wgm_ref, wout_ref, wsp_ref, bsp_ref, o_ref, ygm_ref):
    tm = x_ref.shape[1]
    d_model = x_ref.shape[2]
    yz = (yf_ref[0].astype(F32) + yb_ref[0].astype(F32)) * z_ref[0].astype(F32)
    y_ssd = (yz * _rms_scale(yz) * sg_ref[...]).astype(BF16)
    br_ssd = _dot(y_ssd, wssd_ref[...])
    gch = d_model // GM_GROUPS
    for n in range(tm // GM_CHUNK):
        rs = slice(n * GM_CHUNK, (n + 1) * GM_CHUNK)
        for g in range(GM_GROUPS):
            cs = slice(g * gch, (g + 1) * gch)
            s = _dot(wsp_ref[g], v_ref[0, rs, cs]) + bsp_ref[:, cs]
            ygm_ref[rs, cs] = (u_ref[0, rs, cs].astype(F32) * s).astype(BF16)
    br_gm = _dot(ygm_ref[...], wgm_ref[...])
    merged = (gt_ref[0, :, :d_model].astype(F32) * br_ssd
              + gt_ref[0, :, d_model:].astype(F32) * br_gm).astype(BF16)
    o_ref[0] = x_ref[0] + gate_ref[0] * _dot(merged, wout_ref[...])


def _merge_call(yf, yb, z, u, v, gt, x, gate, ssd_g, w_ssd, w_gm, w_out, w_sp, bsp_exp, *, tm):
    bsz, length, d_model = x.shape
    row = lambda arr: pl.BlockSpec((1, tm, arr.shape[2]), lambda b, i: (b, i, 0))
    vec = pl.BlockSpec((1, 1, d_model), lambda b, i: (b, 0, 0))
    return pl.pallas_call(
        _merge_kernel,
        grid=(bsz, length // tm),
        in_specs=[row(yf), row(yb), row(z), row(u), row(v), row(gt), row(x), vec,
                  _resident(ssd_g.shape), _resident(w_ssd.shape), _resident(w_gm.shape),
                  _resident(w_out.shape), _resident(w_sp.shape), _resident(bsp_exp.shape)],
        out_specs=row(x),
        out_shape=jax.ShapeDtypeStruct(x.shape, F32),
        scratch_shapes=[pltpu.VMEM((tm, d_model), BF16)],
        compiler_params=_cparams(("arbitrary", "arbitrary")),
        name="merge",
    )(yf, yb, z, u, v, gt, x, gate, ssd_g, w_ssd, w_gm, w_out, w_sp, bsp_exp)


FFN_CHUNK = 256


def _ffn_kernel(x_ref, sh_ref, sc_ref, gate_ref, g_ref, wg_ref, wu_ref, wd_ref, o_ref):
    x = x_ref[0]
    h = x * _rms_scale(x) * g_ref[...]
    hb = (h * (1.0 + sc_ref[0]) + sh_ref[0]).astype(BF16)
    d_ff = wg_ref.shape[1]
    acc = jnp.zeros(x.shape, F32)
    step = 4 * FFN_CHUNK
    for c in range(0, d_ff, step):
        wdt = min(step, d_ff - c)
        act = (_silu(_dot(hb, wg_ref[:, c:c + wdt])) * _dot(hb, wu_ref[:, c:c + wdt])).astype(BF16)
        acc = acc + _dot(act, wd_ref[c:c + wdt, :])
    o_ref[0] = x + gate_ref[0] * acc


def _ffn_call(x, shift, scale, gate, n2g, w_gate, w_up, w_down, *, tm):
    bsz, length, d_model = x.shape
    row = pl.BlockSpec((1, tm, d_model), lambda b, i: (b, i, 0))
    vec = pl.BlockSpec((1, 1, d_model), lambda b, i: (b, 0, 0))
    return pl.pallas_call(
        _ffn_kernel,
        grid=(bsz, length // tm),
        in_specs=[row, vec, vec, vec, _resident(n2g.shape), _resident(w_gate.shape),
                  _resident(w_up.shape), _resident(w_down.shape)],
        out_specs=row,
        out_shape=jax.ShapeDtypeStruct(x.shape, F32),
        compiler_params=_cparams(("arbitrary", "arbitrary")),
        name="ffn",
    )(x, shift, scale, gate, n2g, w_gate, w_up, w_down)


MOE_TT = 512
MOE_TM = 512
MOE_TF = 512
RUN_ALIGN = 8
STAGE_ROWS = TOP_K * MOE_TT + N_EXPERTS * (RUN_ALIGN - 1) + 8
RUN_BITS = tuple(range(MOE_TT.bit_length() - 1, 2, -1))
N_PLAN = 3


def _modulated(x_ref, sh_ref, sc_ref, g_ref):
    x = x_ref[0]
    h = x * _rms_scale(x) * g_ref[...]
    return h * (1.0 + sc_ref[0]) + sh_ref[0]


def _router_kernel(x_ref, sh_ref, sc_ref, g_ref, wr_ref, br_ref, p_ref, lpos_ref, lpost_ref, cnt_ref):
    h = _modulated(x_ref, sh_ref, sc_ref, g_ref)
    tt = h.shape[0]
    logits = _dot(h.astype(BF16), wr_ref[...]) + br_ref[...]
    eidx = lax.broadcasted_iota(jnp.int32, logits.shape, 1).astype(F32)
    no_expert = float(N_EXPERTS)
    m1 = jnp.max(logits, axis=-1, keepdims=True)
    i1 = jnp.min(jnp.where(logits == m1, eidx, no_expert), axis=-1, keepdims=True)
    rest = jnp.where(eidx == i1, -jnp.inf, logits)
    m2 = jnp.max(rest, axis=-1, keepdims=True)
    i2 = jnp.min(jnp.where(rest == m2, eidx, no_expert), axis=-1, keepdims=True)
    e2 = jnp.exp(m2 - m1)
    p1 = 1.0 / (1.0 + e2)
    first = lax.broadcasted_iota(jnp.int32, p_ref.shape, 1) == 0
    p_ref[...] = jnp.where(first, p1, e2 * p1)

    oh1 = eidx == i1
    oh2 = eidx == i2
    sel = jnp.where(oh1 | oh2, 1.0, 0.0)
    cnt = jnp.sum(sel, axis=0, keepdims=True)
    cnt_ref[0] = cnt
    run_len = jnp.ceil(cnt * (1.0 / RUN_ALIGN)) * RUN_ALIGN
    lane = lax.broadcasted_iota(jnp.int32, cnt.shape, 1)
    run_off = jnp.zeros_like(cnt)
    for e in range(1, N_EXPERTS):
        before = jnp.sum(jnp.where(lane < e, run_len, 0.0), axis=1, keepdims=True)
        run_off = run_off + jnp.where(lane == e, before, 0.0)
    r = lax.broadcasted_iota(jnp.int32, (tt, tt), 0)
    c = lax.broadcasted_iota(jnp.int32, (tt, tt), 1)
    earlier = jnp.where(c < r, 1.0, 0.0).astype(BF16)
    rank = _dot(earlier, sel.astype(BF16))
    base = run_off + rank
    l1 = jnp.sum(jnp.where(oh1, base, 0.0), axis=1, keepdims=True)
    l2 = jnp.sum(jnp.where(oh2, base, 0.0), axis=1, keepdims=True)
    lpos_ref[...] = jnp.where(first, l1, l2)
    lane_w = lax.broadcasted_iota(jnp.int32, (tt, LANES), 1)
    wide = jnp.where(lane_w == 0, l1, jnp.where(lane_w == 1, l2, 0.0))
    lpost_ref[0] = wide.T[0:8]


def _router_call(x, shift, scale, n2g, w_router, b_router):
    bsz, length, d_model = x.shape
    tt = MOE_TT
    nt = length // tt
    tokens = bsz * length
    row = pl.BlockSpec((1, tt, d_model), lambda b, i: (b, i, 0))
    vec = pl.BlockSpec((1, 1, d_model), lambda b, i: (b, 0, 0))
    flat = lambda width: pl.BlockSpec((tt, width), lambda b, i: (b * nt + i, 0))
    return pl.pallas_call(
        _router_kernel,
        grid=(bsz, nt),
        in_specs=[row, vec, vec, _resident(n2g.shape), _resident(w_router.shape),
                  _resident(b_router.shape)],
        out_specs=[flat(TOP_K), flat(TOP_K),
                   pl.BlockSpec((1, 8, tt), lambda b, i: (b * nt + i, 0, 0)),
                   pl.BlockSpec((1, 1, N_EXPERTS), lambda b, i: (b * nt + i, 0, 0))],
        out_shape=[jax.ShapeDtypeStruct((tokens, TOP_K), F32),
                   jax.ShapeDtypeStruct((tokens, TOP_K), F32),
                   jax.ShapeDtypeStruct((tokens // tt, 8, tt), F32),
                   jax.ShapeDtypeStruct((tokens // tt, 1, N_EXPERTS), F32)],
        compiler_params=_cparams(("arbitrary", "arbitrary")),
        name="router",
    )(x, shift, scale, n2g, w_router, b_router)


def _moe_plan(cnt):
    cnt = cnt.astype(jnp.int32)
    run_len = (cnt + RUN_ALIGN - 1) // RUN_ALIGN * RUN_ALIGN
    rows_e = jnp.sum(run_len, axis=0)
    tiles_e = (rows_e + MOE_TM - 1) // MOE_TM
    tile_end = jnp.cumsum(tiles_e)
    region = (tile_end - tiles_e) * MOE_TM
    run_row = region[None, :] + jnp.cumsum(run_len, axis=0) - run_len
    run_off = jnp.cumsum(run_len, axis=1) - run_len
    n_tiles_max = (cnt.shape[0] * STAGE_ROWS + MOE_TM - 1) // MOE_TM + N_EXPERTS
    tile_gid = jnp.sum(jnp.arange(n_tiles_max, dtype=jnp.int32)[:, None] >= tile_end[None, :], axis=1)
    tile_gid = jnp.minimum(tile_gid, N_EXPERTS - 1).astype(jnp.int32)
    flat = lambda a: a.reshape(-1).astype(jnp.int32)
    plan = (flat(run_row), flat(run_off), flat(run_len))
    fill = (flat(region + rows_e), flat(tiles_e * MOE_TM - rows_e))
    return plan, fill, tile_gid, tile_end[-1:].astype(jnp.int32), n_tiles_max * MOE_TM


def _run_copies(tile, plan_refs, hbm, stage, sem, to_hbm):
    row_ref, off_ref, len_ref = plan_refs
    out = []
    for e in range(N_EXPERTS):
        k = tile * N_EXPERTS + e
        n = len_ref[k]
        for b in RUN_BITS:
            done = (n >> (b + 1)) << (b + 1)
            src = stage.at[pl.ds(pl.multiple_of(off_ref[k] + done, RUN_ALIGN), 1 << b)]
            dst = hbm.at[pl.ds(pl.multiple_of(row_ref[k] + done, RUN_ALIGN), 1 << b)]
            cp = pltpu.make_async_copy(src, dst, sem) if to_hbm else pltpu.make_async_copy(dst, src, sem)
            out.append((((n >> b) & 1) == 1, cp))
    return out


def _start_then_wait(copies):
    for pred, cp in copies:
        pl.when(pred)(cp.start)
    for pred, cp in copies:
        pl.when(pred)(cp.wait)


def _dispatch_kernel(row_ref, off_ref, len_ref, fstart_ref, flen_ref, nt_ref, x_ref, sh_ref, sc_ref, g_ref,
                     lpost_ref, xs_hbm, stage, sem):
    tile = pl.program_id(0) * pl.num_programs(1) + pl.program_id(1)
    hb = _modulated(x_ref, sh_ref, sc_ref, g_ref).astype(BF16)
    tt = hb.shape[0]
    rows = lax.broadcasted_iota(jnp.int32, (STAGE_ROWS, tt), 0).astype(F32)
    lt = lpost_ref[0]
    place = jnp.where((rows == lt[0:1]) | (rows == lt[1:2]), 1.0, 0.0).astype(BF16)
    stage[...] = _dot(place, hb)
    _start_then_wait(_run_copies(tile, (row_ref, off_ref, len_ref), xs_hbm, stage, sem, True))

    @pl.when(tile == pl.num_programs(0) * pl.num_programs(1) - 1)
    def _():
        stage[0:MOE_TM, :] = jnp.zeros((MOE_TM, stage.shape[1]), F32)
        fills = []
        for e in range(N_EXPERTS):
            n = flen_ref[e]
            for b in RUN_BITS[1:]:
                done = (n >> (b + 1)) << (b + 1)
                dst = xs_hbm.at[pl.ds(pl.multiple_of(fstart_ref[e] + done, RUN_ALIGN), 1 << b)]
                fills.append((((n >> b) & 1) == 1, pltpu.make_async_copy(stage.at[pl.ds(0, 1 << b)], dst, sem)))
        _start_then_wait(fills)

        def zero_tile(m, carry):
            cp = pltpu.make_async_copy(stage.at[pl.ds(0, MOE_TM)],
                                       xs_hbm.at[pl.ds(pl.multiple_of(m * MOE_TM, MOE_TM), MOE_TM)], sem)
            cp.start()
            cp.wait()
            return carry

        lax.fori_loop(nt_ref[0], xs_hbm.shape[0] // MOE_TM, zero_tile, 0)


def _dispatch_call(plan, fill, n_tiles, x, shift, scale, n2g, lpost, rows):
    bsz, length, d_model = x.shape
    tt = MOE_TT
    nt = length // tt
    row = pl.BlockSpec((1, tt, d_model), lambda b, i, *_: (b, i, 0))
    vec = pl.BlockSpec((1, 1, d_model), lambda b, i, *_: (b, 0, 0))
    grid_spec = pltpu.PrefetchScalarGridSpec(
        num_scalar_prefetch=N_PLAN + 3,
        grid=(bsz, nt),
        in_specs=[row, vec, vec, pl.BlockSpec(n2g.shape, lambda b, i, *_: (0, 0)),
                  pl.BlockSpec((1, 8, tt), lambda b, i, *_: (b * nt + i, 0, 0))],
        out_specs=pl.BlockSpec(memory_space=pl.ANY),
        scratch_shapes=[pltpu.VMEM((STAGE_ROWS, d_model), F32), pltpu.SemaphoreType.DMA],
    )
    return pl.pallas_call(
        _dispatch_kernel,
        grid_spec=grid_spec,
        out_shape=jax.ShapeDtypeStruct((rows, d_model), F32),
        compiler_params=_cparams(("arbitrary", "arbitrary")),
        name="moe_dispatch",
    )(*plan, *fill, n_tiles, x, shift, scale, n2g, lpost)


def _experts_kernel(gid_ref, nt_ref, xs_ref, wg_ref, wu_ref, wd_ref, ys_ref, xb_ref, acc_ref):
    m = pl.program_id(0)
    f = pl.program_id(1)

    @pl.when(m < nt_ref[0])
    def _():
        @pl.when(f == 0)
        def _():
            xb_ref[...] = xs_ref[...].astype(BF16)

        xb = xb_ref[...]
        act = (_silu(_dot(xb, wg_ref[0])) * _dot(xb, wu_ref[0])).astype(BF16)
        part = _dot(act, wd_ref[0])

        @pl.when(f == 0)
        def _():
            acc_ref[...] = part

        @pl.when(f > 0)
        def _():
            acc_ref[...] += part

        @pl.when(f == pl.num_programs(1) - 1)
        def _():
            ys_ref[...] = acc_ref[...]

    @pl.when(m >= nt_ref[0])
    def _():
        ys_ref[...] = jnp.zeros_like(ys_ref)


def _experts_call(tile_gid, n_tiles, xs, w_gate, w_up, w_down):
    rows, d_model = xs.shape
    d_ff = w_gate.shape[2]
    nt_max = rows // MOE_TM
    nf = d_ff // MOE_TF

    def tile(m, nt):
        return jnp.minimum(m, nt[0] - 1)

    def fcol(m, f, nt):
        return jnp.where(m < nt[0], f, nf - 1)

    grid_spec = pltpu.PrefetchScalarGridSpec(
        num_scalar_prefetch=2,
        grid=(nt_max, nf),
        in_specs=[pl.BlockSpec((MOE_TM, d_model), lambda m, f, gid, nt: (tile(m, nt), 0)),
                  pl.BlockSpec((1, d_model, MOE_TF),
                               lambda m, f, gid, nt: (gid[tile(m, nt)], 0, fcol(m, f, nt))),
                  pl.BlockSpec((1, d_model, MOE_TF),
                               lambda m, f, gid, nt: (gid[tile(m, nt)], 0, fcol(m, f, nt))),
                  pl.BlockSpec((1, MOE_TF, d_model),
                               lambda m, f, gid, nt: (gid[tile(m, nt)], fcol(m, f, nt), 0))],
        out_specs=pl.BlockSpec((MOE_TM, d_model), lambda m, f, gid, nt: (m, 0)),
        scratch_shapes=[pltpu.VMEM((MOE_TM, d_model), BF16), pltpu.VMEM((MOE_TM, d_model), F32)],
    )
    return pl.pallas_call(
        _experts_kernel,
        grid_spec=grid_spec,
        out_shape=jax.ShapeDtypeStruct((rows, d_model), F32),
        compiler_params=_cparams(("arbitrary", "arbitrary")),
        name="moe_experts",
    )(tile_gid, n_tiles, xs, w_gate, w_up, w_down)


def _combine_kernel(row_ref, off_ref, len_ref, ys_hbm, x_ref, p_ref, lpos_ref, gate_ref, fg_ref, o_ref,
                    stage, sem):
    tile = pl.program_id(0) * pl.num_programs(1) + pl.program_id(1)

    @pl.when(tile == 0)
    def _():
        stage[...] = jnp.zeros(stage.shape, F32)

    _start_then_wait(_run_copies(tile, (row_ref, off_ref, len_ref), ys_hbm, stage, sem, False))
    tt = x_ref.shape[1]
    cols = lax.broadcasted_iota(jnp.int32, (tt, STAGE_ROWS), 1).astype(F32)
    p = p_ref[...]
    lp = lpos_ref[...]
    pick = (jnp.where(cols == lp[:, 0:1], p[:, 0:1], 0.0)
            + jnp.where(cols == lp[:, 1:2], p[:, 1:2], 0.0)).astype(BF16)
    y = _dot(pick, stage[...].astype(BF16))
    xn = x_ref[0] + gate_ref[0] * y
    o_ref[0] = xn * _rms_scale(xn) * fg_ref[...]


def _combine_call(plan, ys, x, p, lpos, gate, final_g):
    bsz, length, d_model = x.shape
    tt = MOE_TT
    nt = length // tt
    row = pl.BlockSpec((1, tt, d_model), lambda b, i, *_: (b, i, 0))
    vec = pl.BlockSpec((1, 1, d_model), lambda b, i, *_: (b, 0, 0))
    flat = pl.BlockSpec((tt, TOP_K), lambda b, i, *_: (b * nt + i, 0))
    grid_spec = pltpu.PrefetchScalarGridSpec(
        num_scalar_prefetch=N_PLAN,
        grid=(bsz, nt),
        in_specs=[pl.BlockSpec(memory_space=pl.ANY), row, flat, flat, vec,
                  pl.BlockSpec(final_g.shape, lambda b, i, *_: (0, 0))],
        out_specs=row,
        scratch_shapes=[pltpu.VMEM((STAGE_ROWS, d_model), F32), pltpu.SemaphoreType.DMA],
    )
    return pl.pallas_call(
        _combine_kernel,
        grid_spec=grid_spec,
        out_shape=jax.ShapeDtypeStruct(x.shape, F32),
        compiler_params=_cparams(("arbitrary", "arbitrary")),
        name="moe_combine",
    )(*plan, ys, x, p, lpos, gate, final_g)


def kernel(x, c, ctx, c_ctx, w_ada, b_ada, norm1_g, w_in, conv_w, conv_b, dt_bias, a_log, d_skip,
           ssd_norm_g, w_ssd_br, gm_norm_g, w_spatial, b_spatial, w_gm_br, w_out, norm2_g, ffn_w_gate,
           ffn_w_up, ffn_w_down, moe_router_w, moe_router_b, moe_w_gate, moe_w_up, moe_w_down,
           final_norm_g):
    bsz, seq, d_model = x.shape
    ctx_len = ctx.shape[1]
    depth = w_in.shape[0]
    d_ssd = SSD_HEADS * SSD_HEAD_DIM
    d_xbc = d_ssd + 2 * SSD_GROUPS * SSD_STATE
    d_gm = d_model
    off_dt = d_ssd + d_xbc
    off_u = off_dt + 2 * SSD_HEADS

    rows = 16
    cc = jnp.concatenate([c, c_ctx[None, :], jnp.zeros((rows - bsz - 1, d_model), F32)], axis=0)
    mod = _mod_call(cc, w_ada, b_ada).reshape(depth, rows, N_MOD, d_model)

    head_of_col = jnp.arange(d_ssd, dtype=jnp.int32) // SSD_HEAD_DIM
    e_mat = (jnp.arange(LANES, dtype=jnp.int32)[:, None] == head_of_col[None, :]).astype(BF16)
    st_zero = jnp.zeros((bsz, SSD_GROUPS, SSD_STATE, GROUP_CH), F32)

    for i in range(depth):
        last = i == depth - 1
        mx = lambda k: mod[i, :bsz, k][:, None, :]
        mc = lambda k: jnp.broadcast_to(mod[i, bsz, k][None, None, :], (bsz, 1, d_model))

        wi = w_in[i]
        pad = jnp.zeros((d_model, LANES - SSD_HEADS), F32)
        w_dt = jnp.concatenate([wi[:, off_dt:off_dt + SSD_HEADS], pad,
                                wi[:, off_dt + SSD_HEADS:off_u], pad], axis=1)
        w_full = jnp.concatenate([wi[:, :off_dt], wi[:, off_u:], w_dt], axis=1).astype(BF16)
        zpad = jnp.zeros((LANES - SSD_HEADS,), F32)
        dtb_pad = jnp.concatenate([dt_bias[i, 0], zpad, dt_bias[i, 1], zpad])[None, :]
        alog_pad = jnp.pad(a_log[i], ((0, 0), (0, LANES - SSD_HEADS)))
        dskip_exp = jnp.repeat(d_skip[i], SSD_HEAD_DIM)[None, :]
        n1g = norm1_g[i][None, :]
        n2g = norm2_g[i][None, :]
        cw = conv_w[i]
        cb = conv_b[i][None, :]
        gmg = gm_norm_g[i][None, :]
        in_kw = dict(d_ssd=d_ssd, d_xbc=d_xbc, d_gm=d_gm)
        mix_w = (ssd_norm_g[i][None, :], w_ssd_br[i].astype(BF16), w_gm_br[i].astype(BF16),
                 w_out[i].astype(BF16), w_spatial[i].astype(BF16),
                 jnp.repeat(b_spatial[i].T, d_gm // GM_GROUPS, axis=1))

        if last:
            w_ssd_only = jnp.concatenate([wi[:, d_ssd:off_dt], w_dt], axis=1).astype(BF16)
            xbc_c, dt_c = _in_proj_call(ctx, mc(0), mc(1), n1g, w_ssd_only, cw, cb, dtb_pad, gmg,
                                        seg=ctx_len, tm=ctx_len, ssd_only=True, **in_kw)
            st_f, st_b = _ssd_call(xbc_c, dt_c, alog_pad, e_mat, dskip_exp, st_zero, st_zero,
                                   with_output=False, d_ssd=d_ssd)
        else:
            z_c, xbc_c, dt_c, u_c, v_c, gt_c = _in_proj_call(
                ctx, mc(0), mc(1), n1g, w_full, cw, cb, dtb_pad, gmg, seg=ctx_len, tm=ctx_len,
                ssd_only=False, **in_kw)
            yf_c, yb_c, st_f, st_b = _ssd_call(xbc_c, dt_c, alog_pad, e_mat, dskip_exp, st_zero,
                                               st_zero, with_output=True, d_ssd=d_ssd)
            ctx_mid = _merge_call(yf_c, yb_c, z_c, u_c, v_c, gt_c, ctx, mc(2), *mix_w, tm=ctx_len)

        z, xbc, dt, u, v, gt = _in_proj_call(x, mx(0), mx(1), n1g, w_full, cw, cb, dtb_pad, gmg,
                                             seg=GRID_W, tm=512, ssd_only=False, **in_kw)
        yf, yb, _, _ = _ssd_call(xbc, dt, alog_pad, e_mat, dskip_exp, st_f, st_b, with_output=True,
                                 d_ssd=d_ssd)
        x = _merge_call(yf, yb, z, u, v, gt, x, mx(2), *mix_w, tm=512)

        j = i // 2
        if i % 2 == 0:
            ffn_w = (ffn_w_gate[j].astype(BF16), ffn_w_up[j].astype(BF16), ffn_w_down[j].astype(BF16))
            x = _ffn_call(x, mx(3), mx(4), mx(5), n2g, *ffn_w, tm=512)
            if not last:
                ctx = _ffn_call(ctx_mid, mc(3), mc(4), mc(5), n2g, *ffn_w, tm=ctx_len)
        else:
            assert last, "the routed layer applies the final norm in its combine step"
            p, lpos, lpost, cnt = _router_call(x, mx(3), mx(4), n2g, moe_router_w[j].astype(BF16),
                                               moe_router_b[j][None, :])
            plan, fill, tile_gid, n_tiles, rows_sorted = _moe_plan(cnt[:, 0, :])
            xs = _dispatch_call(plan, fill, n_tiles, x, mx(3), mx(4), n2g, lpost, rows_sorted)
            ys = _experts_call(tile_gid, n_tiles, xs, moe_w_gate[j].astype(BF16),
                               moe_w_up[j].astype(BF16), moe_w_down[j].astype(BF16))
            x = _combine_call(plan, ys, x, p, lpos, mx(5), final_norm_g[None, :])
    return x
```

```python
import functools

import jax
import jax.numpy as jnp
from jax import lax
from jax.experimental import pallas as pl
from jax.experimental.pallas import tpu as pltpu

F32 = jnp.float32
BF16 = jnp.bfloat16
EPS = 1e-6

SSD_HEAD_DIM = 64
SSD_HEADS = 32
SSD_GROUPS = 4
SSD_STATE = 128
HEADS_PER_GROUP = SSD_HEADS // SSD_GROUPS
GROUP_CH = HEADS_PER_GROUP * SSD_HEAD_DIM
CONV_WIDTH = 5
GRID_W = 64
GM_CHUNK = 128
GM_GROUPS = 8
N_EXPERTS = 8
TOP_K = 2
N_MOD = 6

LANES = 128
SSD_Q = 128
NEG_BIG = -1e30
LOG2_E = 1.4426950408889634

VMEM_LIMIT = 56 * 1024 * 1024


def _cparams(sem, vmem=VMEM_LIMIT):
    return pltpu.CompilerParams(dimension_semantics=sem, vmem_limit_bytes=vmem)


def _resident(shape):
    nd = len(shape)
    return pl.BlockSpec(shape, lambda *_: (0,) * nd, pipeline_mode=pl.Buffered(1))


def _sigmoid(v):
    return 0.5 * jnp.tanh(0.5 * v) + 0.5


def _silu(v):
    hv = 0.5 * v
    return hv * jnp.tanh(hv) + hv


def _gelu_tanh(v):
    return 0.5 * v * (1.0 + jnp.tanh(0.7978845608028654 * (v + 0.044715 * v * v * v)))


def _softplus(v):
    return jnp.maximum(v, 0.0) + jnp.log1p(jnp.exp(-jnp.abs(v)))


def _rms_scale(v):
    return lax.rsqrt(jnp.mean(v * v, axis=-1, keepdims=True) + EPS)


def _dot(a, b):
    return jnp.dot(a, b, preferred_element_type=F32)


def _mod_kernel(cc_ref, w_ref, b_ref, o_ref):
    s = _silu(cc_ref[...])
    o_ref[0] = _dot(s.astype(BF16), w_ref[0].astype(BF16)) + b_ref[0]


def _mod_call(cc, w_ada, b_ada):
    depth, d, n = w_ada.shape
    tn = 1536
    rows = cc.shape[0]
    return pl.pallas_call(
        _mod_kernel,
        grid=(depth, n // tn),
        in_specs=[pl.BlockSpec((rows, d), lambda i, j: (0, 0)),
                  pl.BlockSpec((1, d, tn), lambda i, j: (i, 0, j)),
                  pl.BlockSpec((1, 1, tn), lambda i, j: (i, 0, j))],
        out_specs=pl.BlockSpec((1, rows, tn), lambda i, j: (i, 0, j)),
        out_shape=jax.ShapeDtypeStruct((depth, rows, n), F32),
        compiler_params=_cparams(("arbitrary", "arbitrary")),
        name="mod",
    )(cc, w_ada, b_ada.reshape(depth, 1, n))


IN_CHUNK = 512


def _in_proj_kernel(x_ref, sh_ref, sc_ref, g_ref, w_ref, cw_ref, cb_ref, dtb_ref, gmg_ref,
                    *out_refs, seg, ssd_only, d_ssd, d_xbc, d_gm, d_model):
    x = x_ref[0]
    tm = x.shape[0]
    h = x * _rms_scale(x) * g_ref[...]
    h = h * (1.0 + sc_ref[0]) + sh_ref[0]
    hb = h.astype(BF16)

    if ssd_only:
        xbc_ref, dt_ref = out_refs
        off_xbc = 0
    else:
        z_ref, xbc_ref, dt_ref, u_ref, v_ref, gt_ref = out_refs
        off_xbc = d_ssd
        for c in range(0, d_ssd, IN_CHUNK):
            acc = _dot(hb, w_ref[:, c:c + IN_CHUNK])
            z_ref[0, :, c:c + IN_CHUNK] = _silu(acc).astype(BF16)

    groups = tm // 8
    gps = seg // 8
    sub = lax.broadcasted_iota(jnp.int32, (groups, 8, IN_CHUNK), 1)
    zero_g = jnp.zeros((1, 8, IN_CHUNK), F32)
    half = CONV_WIDTH // 2
    for c in range(0, d_xbc, IN_CHUNK):
        cs = slice(c, c + IN_CHUNK)
        acc = _dot(hb, w_ref[:, off_xbc + c:off_xbc + c + IN_CHUNK]).reshape(groups, 8, IN_CHUNK)
        out = cb_ref[:, cs] + cw_ref[half:half + 1, cs] * acc
        for d in range(1, half + 1):
            rot = pltpu.roll(acc, 8 - d, axis=1)
            nxt = jnp.concatenate([p for s0 in range(0, groups, gps)
                                   for p in (rot[s0 + 1:s0 + gps], zero_g)], axis=0)
            ahead = jnp.where(sub < 8 - d, rot, nxt)
            rot = pltpu.roll(acc, d, axis=1)
            prv = jnp.concatenate([p for s0 in range(0, groups, gps)
                                   for p in (zero_g, rot[s0:s0 + gps - 1])], axis=0)
            behind = jnp.where(sub >= d, rot, prv)
            out = out + cw_ref[half + d:half + d + 1, cs] * ahead + cw_ref[half - d:half - d + 1, cs] * behind
        xbc_ref[0, :, cs] = _silu(out).reshape(tm, IN_CHUNK).astype(BF16)

    off = off_xbc + d_xbc
    if not ssd_only:
        for c in range(0, d_gm, IN_CHUNK):
            acc = _dot(hb, w_ref[:, off + c:off + c + IN_CHUNK])
            u_ref[0, :, c:c + IN_CHUNK] = _gelu_tanh(acc).astype(BF16)
        off += d_gm
        gv = _gelu_tanh(_dot(hb, w_ref[:, off:off + d_gm]))
        v_ref[0] = (gv * _rms_scale(gv) * gmg_ref[...]).astype(BF16)
        off += d_gm
        for c in range(0, 2 * d_model, IN_CHUNK):
            acc = _dot(hb, w_ref[:, off + c:off + c + IN_CHUNK])
            gt_ref[0, :, c:c + IN_CHUNK] = _sigmoid(acc).astype(BF16)
        off += 2 * d_model
    dt_ref[0] = _softplus(_dot(hb, w_ref[:, off:off + 2 * LANES]) + dtb_ref[...])


def _in_proj_call(x, shift, scale, n1g, w_perm, conv_w, conv_b, dtb_pad, gm_g, *, seg, tm, ssd_only,
                  d_ssd, d_xbc, d_gm):
    bsz, length, d_model = x.shape
    row = lambda width: pl.BlockSpec((1, tm, width), lambda b, i: (b, i, 0))
    vec = pl.BlockSpec((1, 1, d_model), lambda b, i: (b, 0, 0))
    shp = lambda width, dt: jax.ShapeDtypeStruct((bsz, length, width), dt)
    if ssd_only:
        out_specs = [row(d_xbc), row(2 * LANES)]
        out_shape = [shp(d_xbc, BF16), shp(2 * LANES, F32)]
    else:
        out_specs = [row(d_ssd), row(d_xbc), row(2 * LANES), row(d_gm), row(d_gm), row(2 * d_model)]
        out_shape = [shp(d_ssd, BF16), shp(d_xbc, BF16), shp(2 * LANES, F32), shp(d_gm, BF16),
                     shp(d_gm, BF16), shp(2 * d_model, BF16)]
    kern = functools.partial(_in_proj_kernel, seg=seg, ssd_only=ssd_only, d_ssd=d_ssd, d_xbc=d_xbc,
                             d_gm=d_gm, d_model=d_model)
    return pl.pallas_call(
        kern,
        grid=(bsz, length // tm),
        in_specs=[row(d_model), vec, vec, _resident(n1g.shape), _resident(w_perm.shape),
                  _resident(conv_w.shape), _resident(conv_b.shape), _resident(dtb_pad.shape),
                  _resident(gm_g.shape)],
        out_specs=out_specs,
        out_shape=out_shape,
        compiler_params=_cparams(("arbitrary", "arbitrary")),
        name="in_proj_ssd_only" if ssd_only else "in_proj",
    )(x, shift, scale, n1g, w_perm, conv_w, conv_b, dtb_pad, gm_g)


def _split3(v):
    v1 = v.astype(BF16)
    r1 = v - v1.astype(F32)
    v2 = r1.astype(BF16)
    v3 = (r1 - v2.astype(F32)).astype(BF16)
    return v1, v2, v3


class _Scan:
    pass


def _ssd_prepare(x_ref, dt_ref, aneg, tmat_ref, *, reverse, with_output):
    q = SSD_Q
    sc = _Scan()
    sc.x_ref = x_ref
    dt = dt_ref[0]
    a = dt * aneg
    tmat = tmat_ref[...]
    a1, a2, a3 = _split3(a)
    acs = _dot(tmat, a1) + _dot(tmat, a2) + _dot(tmat, a3)
    a_tot = acs[0:1] if reverse else acs[q - 1:q]
    w = jnp.exp(a_tot - acs) * dt
    eat = jnp.broadcast_to(jnp.exp(a_tot), (8, LANES))
    sc.stack = jnp.concatenate([w, jnp.exp(acs), eat], axis=0).astype(BF16)
    if with_output:
        row = lax.broadcasted_iota(jnp.int32, (q, q), 0)
        col = lax.broadcasted_iota(jnp.int32, (q, q), 1)
        sc.tri = (row <= col) if reverse else (row >= col)
        acs2 = acs * LOG2_E
        sc.col = acs2
        sc.row_t = (jnp.log2(dt) - acs2).T
    return sc


def _ssd_group(sc, g, b_ref, c_ref, e_ref, dskip_ref, s_ref, y_ref):
    q = SSD_Q
    gs = slice(g * GROUP_CH, (g + 1) * GROUP_CH)
    bg = b_ref[0, :, g * SSD_STATE:(g + 1) * SSD_STATE]
    expd = _dot(sc.stack, e_ref[:, gs])
    if y_ref is not None:
        lane = lax.broadcasted_iota(jnp.int32, (q, LANES), 1)
        lo_half = lane < SSD_HEAD_DIM
        cg = c_ref[0, :, g * SSD_STATE:(g + 1) * SSD_STATE]
        cb = lax.dot_general(cg, bg, (((1,), (1,)), ((), ())), preferred_element_type=F32)
        y_off = _dot(cg, s_ref[g].astype(BF16)) * expd[q:2 * q]
        for j in range(HEADS_PER_GROUP // 2):
            h0 = g * HEADS_PER_GROUP + 2 * j
            ms = []
            for hh in (h0, h0 + 1):
                expo = sc.col[:, hh:hh + 1] + sc.row_t[hh:hh + 1, :]
                ms.append((cb * jnp.exp2(jnp.where(sc.tri, expo, NEG_BIG))).astype(BF16))
            lhs = jnp.concatenate(ms, axis=1)
            cs = slice(h0 * SSD_HEAD_DIM, (h0 + 2) * SSD_HEAD_DIM)
            xp = sc.x_ref[0, :, cs]
            zero = jnp.zeros_like(xp)
            rhs = jnp.concatenate([jnp.where(lo_half, xp, zero), jnp.where(lo_half, zero, xp)],
                                  axis=0)
            y = _dot(lhs, rhs) + y_off[:, 2 * j * SSD_HEAD_DIM:(2 * j + 2) * SSD_HEAD_DIM]
            if dskip_ref is not None:
                y = y + xp.astype(F32) * dskip_ref[:, cs]
            y_ref[0, :, cs] = y.astype(y_ref.dtype)
    xw = (sc.x_ref[0, :, gs].astype(F32) * expd[0:q]).astype(BF16)
    new = lax.dot_general(bg, xw, (((0,), (0,)), ((), ())), preferred_element_type=F32)
    s_ref[g] = s_ref[g] * expd[2 * q:2 * q + 1] + new


def _ssd_kernel(xf_ref, bf_ref, cf_ref, dtf_ref, xb_ref, bb_ref, cb_ref, dtb_ref, alog_ref, e_ref,
                dskip_ref, tril_ref, triu_ref, h0f_ref, h0b_ref, *rest, with_output):
    if with_output:
        yf_ref, yb_ref, hff_ref, hfb_ref, sf_ref, sb_ref = rest
    else:
        hff_ref, hfb_ref, sf_ref, sb_ref = rest
        yf_ref = yb_ref = None
    c = pl.program_id(1)

    @pl.when(c == 0)
    def _():
        sf_ref[...] = h0f_ref[0]
        sb_ref[...] = h0b_ref[0]

    lane = lax.broadcasted_iota(jnp.int32, (1, LANES), 1)
    aneg = jnp.where(lane < SSD_HEADS, -jnp.exp(alog_ref[...]), 0.0)
    fwd = _ssd_prepare(xf_ref, dtf_ref, aneg[0:1], tril_ref, reverse=False, with_output=with_output)
    bwd = _ssd_prepare(xb_ref, dtb_ref, aneg[1:2], triu_ref, reverse=True, with_output=with_output)
    for g in range(SSD_GROUPS):
        _ssd_group(fwd, g, bf_ref, cf_ref, e_ref, dskip_ref, sf_ref, yf_ref)
        _ssd_group(bwd, g, bb_ref, cb_ref, e_ref, None, sb_ref, yb_ref)

    @pl.when(c == pl.num_programs(1) - 1)
    def _():
        hff_ref[0] = sf_ref[...]
        hfb_ref[0] = sb_ref[...]


def _ssd_call(xbc, dt, alog_pad, e_mat, dskip_exp, tril, triu, h0f, h0b, *, with_output, d_ssd):
    bsz, length, d_xbc = xbc.shape
    q = SSD_Q
    nc = length // q
    nb = d_ssd // (SSD_GROUPS * SSD_STATE)
    fwd = lambda b, c: c
    bwd = lambda b, c: nc - 1 - c

    def chunk_specs(pos, dt_block):
        return [pl.BlockSpec((1, q, d_ssd), lambda b, c: (b, pos(b, c), 0)),
                pl.BlockSpec((1, q, SSD_GROUPS * SSD_STATE), lambda b, c: (b, pos(b, c), nb)),
                pl.BlockSpec((1, q, SSD_GROUPS * SSD_STATE), lambda b, c: (b, pos(b, c), nb + 1)),
                pl.BlockSpec((1, q, LANES), lambda b, c: (b, pos(b, c), dt_block))]

    st_shape = (SSD_GROUPS, SSD_STATE, GROUP_CH)
    st_spec = pl.BlockSpec((1,) + st_shape, lambda b, c: (b, 0, 0, 0))
    in_specs = (chunk_specs(fwd, 0) + chunk_specs(bwd, 1)
                + [_resident(alog_pad.shape), _resident(e_mat.shape), _resident(dskip_exp.shape),
                   _resident(tril.shape), _resident(triu.shape), st_spec, st_spec])
    st_out = jax.ShapeDtypeStruct((bsz,) + st_shape, F32)
    out_specs = [st_spec, st_spec]
    out_shape = [st_out, st_out]
    if with_output:
        y_out = jax.ShapeDtypeStruct((bsz, length, d_ssd), BF16)
        out_specs = [pl.BlockSpec((1, q, d_ssd), lambda b, c: (b, c, 0)),
                     pl.BlockSpec((1, q, d_ssd), lambda b, c: (b, nc - 1 - c, 0))] + out_specs
        out_shape = [y_out, y_out] + out_shape
    return pl.pallas_call(
        functools.partial(_ssd_kernel, with_output=with_output),
        grid=(bsz, nc),
        in_specs=in_specs,
        out_specs=out_specs,
        out_shape=out_shape,
        scratch_shapes=[pltpu.VMEM(st_shape, F32), pltpu.VMEM(st_shape, F32)],
        compiler_params=_cparams(("arbitrary", "arbitrary")),
        name="ssd" if with_output else "ssd_states",
    )(xbc, xbc, xbc, dt, xbc, xbc, xbc, dt, alog_pad, e_mat, dskip_exp, tril, triu, h0f, h0b)


def _merge_kernel(yf_ref, yb_ref, z_ref, u_ref, v_ref, gt_ref, x_ref, gate_ref, sg_ref, wssd_ref,
                  wgm_ref, wout_ref, wsp_ref, bsp_ref, o_ref, ygm_ref):
    tm = x_ref.shape[1]
    d_model = x_ref.shape[2]
    yz = (yf_ref[0].astype(F32) + yb_ref[0].astype(F32)) * z_ref[0].astype(F32)
    y_ssd = (yz * _rms_scale(yz) * sg_ref[...]).astype(BF16)
    br_ssd = _dot(y_ssd, wssd_ref[...])
    gch = d_model // GM_GROUPS
    for n in range(tm // GM_CHUNK):
        rs = slice(n * GM_CHUNK, (n + 1) * GM_CHUNK)
        for g in range(GM_GROUPS):
            cs = slice(g * gch, (g + 1) * gch)
            s = _dot(wsp_ref[g], v_ref[0, rs, cs]) + bsp_ref[:, cs]
            ygm_ref[rs, cs] = (u_ref[0, rs, cs].astype(F32) * s).astype(BF16)
    br_gm = _dot(ygm_ref[...], wgm_ref[...])
    merged = (gt_ref[0, :, :d_model].astype(F32) * br_ssd
              + gt_ref[0, :, d_model:].astype(F32) * br_gm).astype(BF16)
    o_ref[0] = x_ref[0] + gate_ref[0] * _dot(merged, wout_ref[...])


def _merge_call(yf, yb, z, u, v, gt, x, gate, ssd_g, w_ssd, w_gm, w_out, w_sp, bsp_exp, *, tm):
    bsz, length, d_model = x.shape
    row = lambda arr: pl.BlockSpec((1, tm, arr.shape[2]), lambda b, i: (b, i, 0))
    vec = pl.BlockSpec((1, 1, d_model), lambda b, i: (b, 0, 0))
    return pl.pallas_call(
        _merge_kernel,
        grid=(bsz, length // tm),
        in_specs=[row(yf), row(yb), row(z), row(u), row(v), row(gt), row(x), vec,
                  _resident(ssd_g.shape), _resident(w_ssd.shape), _resident(w_gm.shape),
                  _resident(w_out.shape), _resident(w_sp.shape), _resident(bsp_exp.shape)],
        out_specs=row(x),
        out_shape=jax.ShapeDtypeStruct(x.shape, F32),
        scratch_shapes=[pltpu.VMEM((tm, d_model), BF16)],
        compiler_params=_cparams(("arbitrary", "arbitrary")),
        name="merge",
    )(yf, yb, z, u, v, gt, x, gate, ssd_g, w_ssd, w_gm, w_out, w_sp, bsp_exp)


FFN_CHUNK = 256


def _ffn_kernel(x_ref, sh_ref, sc_ref, gate_ref, g_ref, wg_ref, wu_ref, wd_ref, o_ref):
    x = x_ref[0]
    h = x * _rms_scale(x) * g_ref[...]
    hb = (h * (1.0 + sc_ref[0]) + sh_ref[0]).astype(BF16)
    d_ff = wg_ref.shape[1]
    acc = jnp.zeros(x.shape, F32)
    step = 4 * FFN_CHUNK
    for c in range(0, d_ff, step):
        wdt = min(step, d_ff - c)
        act = (_silu(_dot(hb, wg_ref[:, c:c + wdt])) * _dot(hb, wu_ref[:, c:c + wdt])).astype(BF16)
        acc = acc + _dot(act, wd_ref[c:c + wdt, :])
    o_ref[0] = x + gate_ref[0] * acc


def _ffn_call(x, shift, scale, gate, n2g, w_gate, w_up, w_down, *, tm):
    bsz, length, d_model = x.shape
    row = pl.BlockSpec((1, tm, d_model), lambda b, i: (b, i, 0))
    vec = pl.BlockSpec((1, 1, d_model), lambda b, i: (b, 0, 0))
    return pl.pallas_call(
        _ffn_kernel,
        grid=(bsz, length // tm),
        in_specs=[row, vec, vec, vec, _resident(n2g.shape), _resident(w_gate.shape),
                  _resident(w_up.shape), _resident(w_down.shape)],
        out_specs=row,
        out_shape=jax.ShapeDtypeStruct(x.shape, F32),
        compiler_params=_cparams(("arbitrary", "arbitrary")),
        name="ffn",
    )(x, shift, scale, gate, n2g, w_gate, w_up, w_down)


MOE_TT = 512
MOE_TM = 512
MOE_TF = 1792
RUN_ALIGN = 8
STAGE_ROWS = TOP_K * MOE_TT + N_EXPERTS * (RUN_ALIGN - 1) + 8
RUN_BITS = tuple(range(MOE_TT.bit_length() - 1, 2, -1))
N_PLAN = 3


def _modulated(x_ref, sh_ref, sc_ref, g_ref):
    x = x_ref[0]
    h = x * _rms_scale(x) * g_ref[...]
    return h * (1.0 + sc_ref[0]) + sh_ref[0]


def _router_kernel(x_ref, sh_ref, sc_ref, g_ref, wr_ref, br_ref, p_ref, lpos_ref, lpost_ref, cnt_ref):
    h = _modulated(x_ref, sh_ref, sc_ref, g_ref)
    tt = h.shape[0]
    logits = _dot(h.astype(BF16), wr_ref[...]) + br_ref[...]
    eidx = lax.broadcasted_iota(jnp.int32, logits.shape, 1).astype(F32)
    no_expert = float(N_EXPERTS)
    m1 = jnp.max(logits, axis=-1, keepdims=True)
    i1 = jnp.min(jnp.where(logits == m1, eidx, no_expert), axis=-1, keepdims=True)
    rest = jnp.where(eidx == i1, -jnp.inf, logits)
    m2 = jnp.max(rest, axis=-1, keepdims=True)
    i2 = jnp.min(jnp.where(rest == m2, eidx, no_expert), axis=-1, keepdims=True)
    e2 = jnp.exp(m2 - m1)
    p1 = 1.0 / (1.0 + e2)
    first = lax.broadcasted_iota(jnp.int32, p_ref.shape, 1) == 0
    p_ref[...] = jnp.where(first, p1, e2 * p1)

    oh1 = eidx == i1
    oh2 = eidx == i2
    sel = jnp.where(oh1 | oh2, 1.0, 0.0)
    cnt = jnp.sum(sel, axis=0, keepdims=True)
    cnt_ref[0] = cnt
    run_len = jnp.ceil(cnt * (1.0 / RUN_ALIGN)) * RUN_ALIGN
    lane = lax.broadcasted_iota(jnp.int32, cnt.shape, 1)
    run_off = jnp.zeros_like(cnt)
    for e in range(1, N_EXPERTS):
        before = jnp.sum(jnp.where(lane < e, run_len, 0.0), axis=1, keepdims=True)
        run_off = run_off + jnp.where(lane == e, before, 0.0)
    r = lax.broadcasted_iota(jnp.int32, (tt, tt), 0)
    c = lax.broadcasted_iota(jnp.int32, (tt, tt), 1)
    earlier = jnp.where(c < r, 1.0, 0.0).astype(BF16)
    rank = _dot(earlier, sel.astype(BF16))
    base = run_off + rank
    l1 = jnp.sum(jnp.where(oh1, base, 0.0), axis=1, keepdims=True)
    l2 = jnp.sum(jnp.where(oh2, base, 0.0), axis=1, keepdims=True)
    lpos_ref[...] = jnp.where(first, l1, l2)
    lane_w = lax.broadcasted_iota(jnp.int32, (tt, LANES), 1)
    wide = jnp.where(lane_w == 0, l1, jnp.where(lane_w == 1, l2, 0.0))
    lpost_ref[0] = wide.T[0:8]


def _router_call(x, shift, scale, n2g, w_router, b_router):
    bsz, length, d_model = x.shape
    tt = MOE_TT
    nt = length // tt
    tokens = bsz * length
    row = pl.BlockSpec((1, tt, d_model), lambda b, i: (b, i, 0))
    vec = pl.BlockSpec((1, 1, d_model), lambda b, i: (b, 0, 0))
    flat = lambda width: pl.BlockSpec((tt, width), lambda b, i: (b * nt + i, 0))
    return pl.pallas_call(
        _router_kernel,
        grid=(bsz, nt),
        in_specs=[row, vec, vec, _resident(n2g.shape), _resident(w_router.shape),
                  _resident(b_router.shape)],
        out_specs=[flat(TOP_K), flat(TOP_K),
                   pl.BlockSpec((1, 8, tt), lambda b, i: (b * nt + i, 0, 0)),
                   pl.BlockSpec((1, 1, N_EXPERTS), lambda b, i: (b * nt + i, 0, 0))],
        out_shape=[jax.ShapeDtypeStruct((tokens, TOP_K), F32),
                   jax.ShapeDtypeStruct((tokens, TOP_K), F32),
                   jax.ShapeDtypeStruct((tokens // tt, 8, tt), F32),
                   jax.ShapeDtypeStruct((tokens // tt, 1, N_EXPERTS), F32)],
        compiler_params=_cparams(("arbitrary", "arbitrary")),
        name="router",
    )(x, shift, scale, n2g, w_router, b_router)


def _moe_plan(cnt):
    cnt = cnt.astype(jnp.int32)
    run_len = (cnt + RUN_ALIGN - 1) // RUN_ALIGN * RUN_ALIGN
    rows_e = jnp.sum(run_len, axis=0)
    tiles_e = (rows_e + MOE_TM - 1) // MOE_TM
    tile_end = jnp.cumsum(tiles_e)
    region = (tile_end - tiles_e) * MOE_TM
    run_row = region[None, :] + jnp.cumsum(run_len, axis=0) - run_len
    run_off = jnp.cumsum(run_len, axis=1) - run_len
    n_tiles_max = (cnt.shape[0] * STAGE_ROWS + MOE_TM - 1) // MOE_TM + N_EXPERTS
    tile_gid = jnp.sum(jnp.arange(n_tiles_max, dtype=jnp.int32)[:, None] >= tile_end[None, :], axis=1)
    tile_gid = jnp.minimum(tile_gid, N_EXPERTS - 1).astype(jnp.int32)
    flat = lambda a: a.reshape(-1).astype(jnp.int32)
    plan = (flat(run_row), flat(run_off), flat(run_len))
    fill = (flat(region + rows_e), flat(tiles_e * MOE_TM - rows_e))
    return plan, fill, tile_gid, tile_end[-1:].astype(jnp.int32), n_tiles_max * MOE_TM


def _run_copies(tile, plan_refs, hbm, stage, sem, to_hbm):
    row_ref, off_ref, len_ref = plan_refs
    out = []
    for e in range(N_EXPERTS):
        k = tile * N_EXPERTS + e
        n = len_ref[k]
        for b in RUN_BITS:
            done = (n >> (b + 1)) << (b + 1)
            src = stage.at[pl.ds(pl.multiple_of(off_ref[k] + done, RUN_ALIGN), 1 << b)]
            dst = hbm.at[pl.ds(pl.multiple_of(row_ref[k] + done, RUN_ALIGN), 1 << b)]
            cp = pltpu.make_async_copy(src, dst, sem) if to_hbm else pltpu.make_async_copy(dst, src, sem)
            out.append((((n >> b) & 1) == 1, cp))
    return out


def _start_then_wait(copies):
    for pred, cp in copies:
        pl.when(pred)(cp.start)
    for pred, cp in copies:
        pl.when(pred)(cp.wait)


def _dispatch_kernel(row_ref, off_ref, len_ref, fstart_ref, flen_ref, nt_ref, x_ref, sh_ref, sc_ref, g_ref,
                     lpost_ref, xs_hbm, stage, sem):
    tile = pl.program_id(0) * pl.num_programs(1) + pl.program_id(1)
    hb = _modulated(x_ref, sh_ref, sc_ref, g_ref).astype(BF16)
    tt = hb.shape[0]
    rows = lax.broadcasted_iota(jnp.int32, (STAGE_ROWS, tt), 0).astype(F32)
    lt = lpost_ref[0]
    place = jnp.where((rows == lt[0:1]) | (rows == lt[1:2]), 1.0, 0.0).astype(BF16)
    stage[...] = _dot(place, hb)
    _start_then_wait(_run_copies(tile, (row_ref, off_ref, len_ref), xs_hbm, stage, sem, True))

    @pl.when(tile == pl.num_programs(0) * pl.num_programs(1) - 1)
    def _():
        stage[0:MOE_TM, :] = jnp.zeros((MOE_TM, stage.shape[1]), F32)
        fills = []
        for e in range(N_EXPERTS):
            n = flen_ref[e]
            for b in RUN_BITS[1:]:
                done = (n >> (b + 1)) << (b + 1)
                dst = xs_hbm.at[pl.ds(pl.multiple_of(fstart_ref[e] + done, RUN_ALIGN), 1 << b)]
                fills.append((((n >> b) & 1) == 1, pltpu.make_async_copy(stage.at[pl.ds(0, 1 << b)], dst, sem)))
        _start_then_wait(fills)

        def zero_tile(m, carry):
            cp = pltpu.make_async_copy(stage.at[pl.ds(0, MOE_TM)],
                                       xs_hbm.at[pl.ds(pl.multiple_of(m * MOE_TM, MOE_TM), MOE_TM)], sem)
            cp.start()
            cp.wait()
            return carry

        lax.fori_loop(nt_ref[0], xs_hbm.shape[0] // MOE_TM, zero_tile, 0)


def _dispatch_call(plan, fill, n_tiles, x, shift, scale, n2g, lpost, rows):
    bsz, length, d_model = x.shape
    tt = MOE_TT
    nt = length // tt
    row = pl.BlockSpec((1, tt, d_model), lambda b, i, *_: (b, i, 0))
    vec = pl.BlockSpec((1, 1, d_model), lambda b, i, *_: (b, 0, 0))
    grid_spec = pltpu.PrefetchScalarGridSpec(
        num_scalar_prefetch=N_PLAN + 3,
        grid=(bsz, nt),
        in_specs=[row, vec, vec, pl.BlockSpec(n2g.shape, lambda b, i, *_: (0, 0)),
                  pl.BlockSpec((1, 8, tt), lambda b, i, *_: (b * nt + i, 0, 0))],
        out_specs=pl.BlockSpec(memory_space=pl.ANY),
        scratch_shapes=[pltpu.VMEM((STAGE_ROWS, d_model), F32), pltpu.SemaphoreType.DMA],
    )
    return pl.pallas_call(
        _dispatch_kernel,
        grid_spec=grid_spec,
        out_shape=jax.ShapeDtypeStruct((rows, d_model), F32),
        compiler_params=_cparams(("arbitrary", "arbitrary")),
        name="moe_dispatch",
    )(*plan, *fill, n_tiles, x, shift, scale, n2g, lpost)


def _experts_kernel(gid_ref, nt_ref, xs_ref, wg_ref, wu_ref, wd_ref, ys_ref, xb_ref, acc_ref):
    m = pl.program_id(0)
    f = pl.program_id(1)

    @pl.when(m < nt_ref[0])
    def _():
        @pl.when(f == 0)
        def _():
            xb_ref[...] = xs_ref[...].astype(BF16)

        xb = xb_ref[...]
        act = (_silu(_dot(xb, wg_ref[0])) * _dot(xb, wu_ref[0])).astype(BF16)
        part = _dot(act, wd_ref[0])

        @pl.when(f == 0)
        def _():
            acc_ref[...] = part

        @pl.when(f > 0)
        def _():
            acc_ref[...] += part

        @pl.when(f == pl.num_programs(1) - 1)
        def _():
            ys_ref[...] = acc_ref[...]

    @pl.when(m >= nt_ref[0])
    def _():
        ys_ref[...] = jnp.zeros_like(ys_ref)


def _experts_call(tile_gid, n_tiles, xs, w_gate, w_up, w_down):
    rows, d_model = xs.shape
    d_ff = w_gate.shape[2]
    assert rows % MOE_TM == 0 and d_ff % MOE_TF == 0, (rows, d_ff)
    nt_max = rows // MOE_TM
    nf = d_ff // MOE_TF

    def tile(m, nt):
        return jnp.minimum(m, nt[0] - 1)

    def fcol(m, f, nt):
        return jnp.where(m < nt[0], f, nf - 1)

    grid_spec = pltpu.PrefetchScalarGridSpec(
        num_scalar_prefetch=2,
        grid=(nt_max, nf),
        in_specs=[pl.BlockSpec((MOE_TM, d_model), lambda m, f, gid, nt: (tile(m, nt), 0)),
                  pl.BlockSpec((1, d_model, MOE_TF),
                               lambda m, f, gid, nt: (gid[tile(m, nt)], 0, fcol(m, f, nt))),
                  pl.BlockSpec((1, d_model, MOE_TF),
                               lambda m, f, gid, nt: (gid[tile(m, nt)], 0, fcol(m, f, nt))),
                  pl.BlockSpec((1, MOE_TF, d_model),
                               lambda m, f, gid, nt: (gid[tile(m, nt)], fcol(m, f, nt), 0))],
        out_specs=pl.BlockSpec((MOE_TM, d_model), lambda m, f, gid, nt: (m, 0)),
        scratch_shapes=[pltpu.VMEM((MOE_TM, d_model), BF16), pltpu.VMEM((MOE_TM, d_model), F32)],
    )
    return pl.pallas_call(
        _experts_kernel,
        grid_spec=grid_spec,
        out_shape=jax.ShapeDtypeStruct((rows, d_model), F32),
        compiler_params=_cparams(("arbitrary", "arbitrary")),
        name="moe_experts",
    )(tile_gid, n_tiles, xs, w_gate, w_up, w_down)


def _combine_kernel(row_ref, off_ref, len_ref, ys_hbm, x_ref, p_ref, lpos_ref, gate_ref, fg_ref, o_ref,
                    stage, sem):
    tile = pl.program_id(0) * pl.num_programs(1) + pl.program_id(1)

    @pl.when(tile == 0)
    def _():
        stage[...] = jnp.zeros(stage.shape, F32)

    _start_then_wait(_run_copies(tile, (row_ref, off_ref, len_ref), ys_hbm, stage, sem, False))
    tt = x_ref.shape[1]
    cols = lax.broadcasted_iota(jnp.int32, (tt, STAGE_ROWS), 1).astype(F32)
    p = p_ref[...]
    lp = lpos_ref[...]
    pick = (jnp.where(cols == lp[:, 0:1], p[:, 0:1], 0.0)
            + jnp.where(cols == lp[:, 1:2], p[:, 1:2], 0.0)).astype(BF16)
    y = _dot(pick, stage[...].astype(BF16))
    xn = x_ref[0] + gate_ref[0] * y
    o_ref[0] = xn * _rms_scale(xn) * fg_ref[...]


def _combine_call(plan, ys, x, p, lpos, gate, final_g):
    bsz, length, d_model = x.shape
    tt = MOE_TT
    nt = length // tt
    row = pl.BlockSpec((1, tt, d_model), lambda b, i, *_: (b, i, 0))
    vec = pl.BlockSpec((1, 1, d_model), lambda b, i, *_: (b, 0, 0))
    flat = pl.BlockSpec((tt, TOP_K), lambda b, i, *_: (b * nt + i, 0))
    grid_spec = pltpu.PrefetchScalarGridSpec(
        num_scalar_prefetch=N_PLAN,
        grid=(bsz, nt),
        in_specs=[pl.BlockSpec(memory_space=pl.ANY), row, flat, flat, vec,
                  pl.BlockSpec(final_g.shape, lambda b, i, *_: (0, 0))],
        out_specs=row,
        scratch_shapes=[pltpu.VMEM((STAGE_ROWS, d_model), F32), pltpu.SemaphoreType.DMA],
    )
    return pl.pallas_call(
        _combine_kernel,
        grid_spec=grid_spec,
        out_shape=jax.ShapeDtypeStruct(x.shape, F32),
        compiler_params=_cparams(("arbitrary", "arbitrary")),
        name="moe_combine",
    )(*plan, ys, x, p, lpos, gate, final_g)


def kernel(x, c, ctx, c_ctx, w_ada, b_ada, norm1_g, w_in, conv_w, conv_b, dt_bias, a_log, d_skip,
           ssd_norm_g, w_ssd_br, gm_norm_g, w_spatial, b_spatial, w_gm_br, w_out, norm2_g, ffn_w_gate,
           ffn_w_up, ffn_w_down, moe_router_w, moe_router_b, moe_w_gate, moe_w_up, moe_w_down,
           final_norm_g):
    bsz, seq, d_model = x.shape
    ctx_len = ctx.shape[1]
    depth = w_in.shape[0]
    d_ssd = SSD_HEADS * SSD_HEAD_DIM
    d_xbc = d_ssd + 2 * SSD_GROUPS * SSD_STATE
    d_gm = d_model
    off_dt = d_ssd + d_xbc
    off_u = off_dt + 2 * SSD_HEADS

    rows = 16
    cc = jnp.concatenate([c, c_ctx[None, :], jnp.zeros((rows - bsz - 1, d_model), F32)], axis=0)
    mod = _mod_call(cc, w_ada, b_ada).reshape(depth, rows, N_MOD, d_model)

    head_of_col = jnp.arange(d_ssd, dtype=jnp.int32) // SSD_HEAD_DIM
    e_mat = (jnp.arange(LANES, dtype=jnp.int32)[:, None] == head_of_col[None, :]).astype(BF16)
    st_zero = jnp.zeros((bsz, SSD_GROUPS, SSD_STATE, GROUP_CH), F32)
    tril = jnp.tril(jnp.ones((SSD_Q, SSD_Q), BF16))

    for i in range(depth):
        last = i == depth - 1
        mx = lambda k: mod[i, :bsz, k][:, None, :]
        mc = lambda k: jnp.broadcast_to(mod[i, bsz, k][None, None, :], (bsz, 1, d_model))

        wi = w_in[i]
        pad = jnp.zeros((d_model, LANES - SSD_HEADS), F32)
        w_dt = jnp.concatenate([wi[:, off_dt:off_dt + SSD_HEADS], pad,
                                wi[:, off_dt + SSD_HEADS:off_u], pad], axis=1)
        w_full = jnp.concatenate([wi[:, :off_dt], wi[:, off_u:], w_dt], axis=1).astype(BF16)
        zpad = jnp.zeros((LANES - SSD_HEADS,), F32)
        dtb_pad = jnp.concatenate([dt_bias[i, 0], zpad, dt_bias[i, 1], zpad])[None, :]
        alog_pad = jnp.pad(a_log[i], ((0, 0), (0, LANES - SSD_HEADS)))
        dskip_exp = jnp.repeat(d_skip[i], SSD_HEAD_DIM)[None, :]
        n1g = norm1_g[i][None, :]
        n2g = norm2_g[i][None, :]
        cw = conv_w[i]
        cb = conv_b[i][None, :]
        gmg = gm_norm_g[i][None, :]
        in_kw = dict(d_ssd=d_ssd, d_xbc=d_xbc, d_gm=d_gm)
        mix_w = (ssd_norm_g[i][None, :], w_ssd_br[i].astype(BF16), w_gm_br[i].astype(BF16),
                 w_out[i].astype(BF16), w_spatial[i].astype(BF16),
                 jnp.repeat(b_spatial[i].T, d_gm // GM_GROUPS, axis=1))

        if last:
            w_ssd_only = jnp.concatenate([wi[:, d_ssd:off_dt], w_dt], axis=1).astype(BF16)
            xbc_c, dt_c = _in_proj_call(ctx, mc(0), mc(1), n1g, w_ssd_only, cw, cb, dtb_pad, gmg,
                                        seg=ctx_len, tm=ctx_len, ssd_only=True, **in_kw)
            st_f, st_b = _ssd_call(xbc_c, dt_c, alog_pad, e_mat, dskip_exp, tril, tril.T, st_zero, st_zero,
                                   with_output=False, d_ssd=d_ssd)
        else:
            z_c, xbc_c, dt_c, u_c, v_c, gt_c = _in_proj_call(
                ctx, mc(0), mc(1), n1g, w_full, cw, cb, dtb_pad, gmg, seg=ctx_len, tm=ctx_len,
                ssd_only=False, **in_kw)
            yf_c, yb_c, st_f, st_b = _ssd_call(xbc_c, dt_c, alog_pad, e_mat, dskip_exp, tril, tril.T,
                                               st_zero, st_zero, with_output=True, d_ssd=d_ssd)
            ctx_mid = _merge_call(yf_c, yb_c, z_c, u_c, v_c, gt_c, ctx, mc(2), *mix_w, tm=ctx_len)

        z, xbc, dt, u, v, gt = _in_proj_call(x, mx(0), mx(1), n1g, w_full, cw, cb, dtb_pad, gmg,
                                             seg=GRID_W, tm=512, ssd_only=False, **in_kw)
        yf, yb, _, _ = _ssd_call(xbc, dt, alog_pad, e_mat, dskip_exp, tril, tril.T, st_f, st_b, with_output=True,
                                 d_ssd=d_ssd)
        x = _merge_call(yf, yb, z, u, v, gt, x, mx(2), *mix_w, tm=512)

        j = i // 2
        if i % 2 == 0:
            ffn_w = (ffn_w_gate[j].astype(BF16), ffn_w_up[j].astype(BF16), ffn_w_down[j].astype(BF16))
            x = _ffn_call(x, mx(3), mx(4), mx(5), n2g, *ffn_w, tm=512)
            if not last:
                ctx = _ffn_call(ctx_mid, mc(3), mc(4), mc(5), n2g, *ffn_w, tm=ctx_len)
        else:
            assert last, "the routed layer applies the final norm in its combine step"
            p, lpos, lpost, cnt = _router_call(x, mx(3), mx(4), n2g, moe_router_w[j].astype(BF16),
                                               moe_router_b[j][None, :])
            plan, fill, tile_gid, n_tiles, rows_sorted = _moe_plan(cnt[:, 0, :])
            xs = _dispatch_call(plan, fill, n_tiles, x, mx(3), mx(4), n2g, lpost, rows_sorted)
            ys = _experts_call(tile_gid, n_tiles, xs, moe_w_gate[j].astype(BF16),
                               moe_w_up[j].astype(BF16), moe_w_down[j].astype(BF16))
            x = _combine_call(plan, ys, x, p, lpos, mx(5), final_norm_g[None, :])
    return x
```
